```python
import jax, jax.numpy as jnp
from jax import lax
import numpy as np

D_MODEL = 2048
BATCH = 8
SEQ = 2048
DEPTH = 4

CHUNK = 64
QBLK = 128
EPS = 1e-6
A_HEADS = 8
A_NOPE = 128
A_ROPE = 64
A_VDIM = 128
A_QLORA = 512
A_KVLORA = 256
ROPE_THETA = 10000.0
B_HEADS = 8
B_HDIM = 128
B_PREV = 8
B_BAND = (B_PREV + 1) * CHUNK
REL_CLIP = 128
SG_WIDTH = D_MODEL
SG_GROUPS = 8
SG_LEN = 128
A_WIDTH = A_HEADS * A_VDIM
B_WIDTH = B_HEADS * B_HDIM
AB_WIDTH = A_WIDTH + B_WIDTH
AB_IN = A_QLORA + A_KVLORA + A_ROPE + 3 * B_WIDTH + AB_WIDTH
N_EVEN = (DEPTH + 1) // 2
N_ODD = DEPTH // 2

kernel_name = "hybrid_mla_bandattn_gmlp_sandwich_adaln"


def rmsnorm(x, g):
    xf = x.astype(jnp.float32)
    y = xf * lax.rsqrt(jnp.mean(xf * xf, axis=-1, keepdims=True) + EPS)
    return y.astype(x.dtype) * g


def layernorm(x, g, b):
    xf = x.astype(jnp.float32)
    mu = jnp.mean(xf, axis=-1, keepdims=True)
    var = jnp.mean(jnp.square(xf - mu), axis=-1, keepdims=True)
    return ((xf - mu) * lax.rsqrt(var + EPS)).astype(x.dtype) * g + b


def rope(x, pos):
    half = x.shape[-1] // 2
    freqs = ROPE_THETA ** (-jnp.arange(half, dtype=jnp.float32) / half)
    ang = pos[:, None] * freqs[None, :]
    cos = jnp.cos(ang)[:, None, :].astype(x.dtype)
    sin = jnp.sin(ang)[:, None, :].astype(x.dtype)
    x1, x2 = x[..., :half], x[..., half:]
    return jnp.concatenate([x1 * cos - x2 * sin, x1 * sin + x2 * cos], axis=-1)


def chunk_causal_attention(q, k, v):
    bsz, s_len, h, dk = q.shape
    nb = s_len // QBLK
    scale = dk ** -0.5
    qb = q.reshape(bsz, nb, QBLK, h, dk).swapaxes(0, 1)
    kchunk = jnp.arange(s_len) // CHUNK

    def one(args):
        qi, i = args
        s = jnp.einsum('bqhd,bkhd->bhqk', qi, k, preferred_element_type=jnp.float32) * scale
        qchunk = (i * QBLK + jnp.arange(QBLK)) // CHUNK
        mask = kchunk[None, :] <= qchunk[:, None]
        p = jax.nn.softmax(jnp.where(mask, s, -jnp.inf), axis=-1)
        return jnp.einsum('bhqk,bkhd->bqhd', p.astype(v.dtype), v)

    o = lax.map(one, (qb, jnp.arange(nb)))
    return o.swapaxes(0, 1).reshape(bsz, s_len, h, v.shape[-1])


def chunk_band_attention(q, k, v, rel_table):
    bsz, s_len, h, d = q.shape
    nc = s_len // CHUNK
    scale = d ** -0.5
    pad = ((0, 0), (B_PREV * CHUNK, 0), (0, 0), (0, 0))
    kp = jnp.pad(k, pad)
    vp = jnp.pad(v, pad)
    rel = (B_PREV * CHUNK + jnp.arange(CHUNK))[:, None] - jnp.arange(B_BAND)[None, :]
    bias = rel_table[:, jnp.clip(rel, -REL_CLIP, REL_CLIP) + REL_CLIP].astype(jnp.float32)

    def one(i):
        qi = lax.dynamic_slice_in_dim(q, i * CHUNK, CHUNK, axis=1)
        kb = lax.dynamic_slice_in_dim(kp, i * CHUNK, B_BAND, axis=1)
        vb = lax.dynamic_slice_in_dim(vp, i * CHUNK, B_BAND, axis=1)
        s = jnp.einsum('bqhd,bkhd->bhqk', qi, kb, preferred_element_type=jnp.float32) * scale + bias[None]
        valid = (i - B_PREV) * CHUNK + jnp.arange(B_BAND) >= 0
        p = jax.nn.softmax(jnp.where(valid[None, None, None, :], s, -jnp.inf), axis=-1)
        return jnp.einsum('bhqk,bkhd->bqhd', p.astype(vb.dtype), vb)

    o = lax.map(one, jnp.arange(nc))
    return o.swapaxes(0, 1).reshape(bsz, s_len, h, d)


def mla_band_mixer(h, w_in, g_q, w_uq, g_kv, w_ukv, rel_table, w_out, pos):
    bsz, s_len, _ = h.shape
    z = h @ w_in
    o1 = A_QLORA
    o2 = o1 + A_KVLORA
    o3 = o2 + A_ROPE
    o4 = o3 + B_WIDTH
    o5 = o4 + B_WIDTH
    o6 = o5 + B_WIDTH
    cq, ckv, kr, bq, bk, bv, gate = jnp.split(z, [o1, o2, o3, o4, o5, o6], axis=-1)
    q = (rmsnorm(cq, g_q) @ w_uq).reshape(bsz, s_len, A_HEADS, A_NOPE + A_ROPE)
    q = jnp.concatenate([q[..., :A_NOPE], rope(q[..., A_NOPE:], pos)], axis=-1)
    kv = (rmsnorm(ckv, g_kv) @ w_ukv).reshape(bsz, s_len, A_HEADS, A_NOPE + A_VDIM)
    kr = rope(kr[:, :, None, :], pos)
    k = jnp.concatenate([kv[..., :A_NOPE], jnp.broadcast_to(kr, (bsz, s_len, A_HEADS, A_ROPE))], axis=-1)
    oa = chunk_causal_attention(q, k, kv[..., A_NOPE:]).reshape(bsz, s_len, A_WIDTH)
    shp = (bsz, s_len, B_HEADS, B_HDIM)
    ob = chunk_band_attention(bq.reshape(shp), bk.reshape(shp), bv.reshape(shp), rel_table)
    ob = ob.reshape(bsz, s_len, B_WIDTH)
    y = jnp.concatenate([oa, ob], axis=-1) * jax.nn.silu(gate)
    return y @ w_out


def spatial_gating_mixer(h, w_in, ln_g, ln_b, w_s, b_s, w_out):
    bsz, s_len, _ = h.shape
    u, v, gate = jnp.split(h @ w_in, 3, axis=-1)
    v = layernorm(v, ln_g, ln_b)
    n = s_len // SG_LEN
    dg = SG_WIDTH // SG_GROUPS
    v = v.reshape(bsz, n, SG_LEN, SG_GROUPS, dg)
    cpos = jnp.arange(SG_LEN) // CHUNK
    mask = cpos[None, :] <= cpos[:, None]
    ws = jnp.where(mask[None], w_s, jnp.zeros((), w_s.dtype))
    sv = jnp.einsum('gij,bnjgd->bnigd', ws, v) + b_s.T[:, :, None]
    y = u * sv.reshape(bsz, s_len, SG_WIDTH) * jax.nn.silu(gate)
    return y @ w_out


def setup_inputs(seed: int = 0) -> dict:
    key = jax.random.key(seed)
    ks = jax.random.split(key, 20)
    f32 = jnp.float32
    nrm = lambda k, shp, s: jax.random.normal(k, shp, f32) * s
    return {
        "x": nrm(ks[0], (BATCH, SEQ, D_MODEL), 1.0),
        "c": nrm(ks[1], (BATCH, D_MODEL), 1.0),
        "w_mod": nrm(ks[2], (DEPTH, D_MODEL, 3 * D_MODEL), 0.5 * D_MODEL ** -0.5),
        "b_mod": nrm(ks[3], (DEPTH, 3 * D_MODEL), 0.01),
        "g_pre": 1.0 + nrm(ks[4], (DEPTH, D_MODEL), 0.05),
        "g_post": 1.0 + nrm(ks[5], (DEPTH, D_MODEL), 0.05),
        "ab_w_in": nrm(ks[6], (N_EVEN, D_MODEL, AB_IN), D_MODEL ** -0.5),
        "a_g_q": 1.0 + nrm(ks[7], (N_EVEN, A_QLORA), 0.05),
        "a_w_uq": nrm(ks[8], (N_EVEN, A_QLORA, A_HEADS * (A_NOPE + A_ROPE)), A_QLORA ** -0.5),
        "a_g_kv": 1.0 + nrm(ks[9], (N_EVEN, A_KVLORA), 0.05),
        "a_w_ukv": nrm(ks[10], (N_EVEN, A_KVLORA, A_HEADS * (A_NOPE + A_VDIM)), A_KVLORA ** -0.5),
        "b_rel_bias": nrm(ks[11], (N_EVEN, B_HEADS, 2 * REL_CLIP + 1), 0.5),
        "ab_w_out": nrm(ks[12], (N_EVEN, AB_WIDTH, D_MODEL), AB_WIDTH ** -0.5),
        "sg_w_in": nrm(ks[13], (N_ODD, D_MODEL, 3 * SG_WIDTH), D_MODEL ** -0.5),
        "sg_ln_g": 1.0 + nrm(ks[14], (N_ODD, SG_WIDTH), 0.05),
        "sg_ln_b": nrm(ks[15], (N_ODD, SG_WIDTH), 0.01),
        "sg_w_s": nrm(ks[16], (N_ODD, SG_GROUPS, SG_LEN, SG_LEN), 0.5 * SG_LEN ** -0.5),
        "sg_b_s": 1.0 + nrm(ks[17], (N_ODD, SG_GROUPS, SG_LEN), 0.05),
        "sg_w_out": nrm(ks[18], (N_ODD, SG_WIDTH, D_MODEL), SG_WIDTH ** -0.5),
    }


def reference(x, c, w_mod, b_mod, g_pre, g_post, ab_w_in, a_g_q, a_w_uq, a_g_kv, a_w_ukv,
              b_rel_bias, ab_w_out, sg_w_in, sg_ln_g, sg_ln_b, sg_w_s, sg_b_s, sg_w_out):
    pos = jnp.arange(x.shape[1], dtype=jnp.float32)
    cs = jax.nn.silu(c)
    for l in range(DEPTH):
        mod = cs @ w_mod[l] + b_mod[l]
        shift, scale, gate = jnp.split(mod[:, None, :], 3, axis=-1)
        h = rmsnorm(x, g_pre[l]) * (1 + scale) + shift
        i = l // 2
        if l % 2 == 0:
            y = mla_band_mixer(h, ab_w_in[i], a_g_q[i], a_w_uq[i], a_g_kv[i], a_w_ukv[i],
                               b_rel_bias[i], ab_w_out[i], pos)
        else:
            y = spatial_gating_mixer(h, sg_w_in[i], sg_ln_g[i], sg_ln_b[i], sg_w_s[i],
                                     sg_b_s[i], sg_w_out[i])
        x = x + gate * rmsnorm(y, g_post[l])
    return x
```

```python
import functools

import jax
import jax.numpy as jnp
from jax import lax
from jax.experimental import pallas as pl
from jax.experimental.pallas import tpu as pltpu

D_MODEL = 2048
DEPTH = 4
CHUNK = 64
EPS = 1e-6
A_HEADS = 8
A_NOPE = 128
A_ROPE = 64
A_VDIM = 128
A_QLORA = 512
A_KVLORA = 256
ROPE_THETA = 10000.0
B_HEADS = 8
B_HDIM = 128
B_PREV = 8
REL_CLIP = 128
SG_GROUPS = 8
SG_LEN = 128
B_WIDTH = B_HEADS * B_HDIM
A_WIDTH = A_HEADS * A_VDIM

VMEM_LIMIT_BYTES = 56 * 1024 * 1024

LATENT_W = 1024
Z_W = 6144
A_QBLK = 256
ATT_TQ = 256
BAND_KEYS = ATT_TQ + B_PREV * CHUNK

BF16 = jnp.bfloat16
F32 = jnp.float32


def _params(*sem):
    return pltpu.CompilerParams(dimension_semantics=sem, vmem_limit_bytes=VMEM_LIMIT_BYTES)


def _dot(a, b):
    return jnp.dot(a, b, preferred_element_type=F32)


def _dot_nt(a, b):
    return lax.dot_general(a, b, (((1,), (1,)), ((), ())), preferred_element_type=F32)


def _rms(x):
    return x * lax.rsqrt(jnp.mean(x * x, axis=-1, keepdims=True) + EPS)


def _silu(x):
    return x * jax.nn.sigmoid(x)


def _mod_kernel(c_ref, w_ref, b_ref, o_ref):
    cs = _silu(c_ref[...]).astype(BF16)
    o_ref[...] = _dot(cs, w_ref[...].astype(BF16)) + b_ref[...]


def _modulation(c, w_mod, b_mod, tn=512):
    depth, d, n = w_mod.shape
    bsz = c.shape[0]
    return pl.pallas_call(
        _mod_kernel,
        grid=(depth, n // tn),
        in_specs=[
            pl.BlockSpec((bsz, d), lambda l, j: (0, 0)),
            pl.BlockSpec((None, d, tn), lambda l, j: (l, 0, j)),
            pl.BlockSpec((None, 1, tn), lambda l, j: (l, 0, j)),
        ],
        out_specs=pl.BlockSpec((None, bsz, tn), lambda l, j: (l, 0, j)),
        out_shape=jax.ShapeDtypeStruct((depth, bsz, n), F32),
        compiler_params=_params("arbitrary", "arbitrary"),
        name="modulation",
    )(c, w_mod, b_mod.reshape(depth, 1, n))


def _inproj_kernel(x_ref, g_ref, scale_ref, shift_ref, w_ref, o_ref, h_ref):
    @pl.when(pl.program_id(1) == 0)
    def _():
        h = _rms(x_ref[...]) * g_ref[...] * (1.0 + scale_ref[...]) + shift_ref[...]
        h_ref[...] = h.astype(BF16)

    o_ref[...] = _dot(h_ref[...], w_ref[...]).astype(BF16)


def _inproj(x2, g_pre, scale, shift, w, seq, tm=1024, tn=1024):
    t, d = x2.shape
    n = w.shape[1]
    per_b = seq // tm
    vec = pl.BlockSpec((None, 1, d), lambda m, j: (m // per_b, 0, 0))
    return pl.pallas_call(
        _inproj_kernel,
        grid=(t // tm, n // tn),
        in_specs=[
            pl.BlockSpec((tm, d), lambda m, j: (m, 0)),
            pl.BlockSpec((1, d), lambda m, j: (0, 0)),
            vec, vec,
            pl.BlockSpec((d, tn), lambda m, j: (0, j)),
        ],
        out_specs=pl.BlockSpec((tm, tn), lambda m, j: (m, j)),
        out_shape=jax.ShapeDtypeStruct((t, n), BF16),
        scratch_shapes=[pltpu.VMEM((tm, d), BF16)],
        compiler_params=_params("arbitrary", "arbitrary"),
        name="inproj",
    )(x2, g_pre, scale, shift, w)


def _latent_kernel(z_ref, gq_ref, wq_ref, gkv_ref, wkv_ref, cs_ref,
                   q_ref, kn_ref, v_ref, kr_ref):
    z = z_ref[...].astype(F32)
    cs = cs_ref[...]
    cq = (_rms(z[:, :A_QLORA]) * gq_ref[...]).astype(BF16)
    q = _dot(cq, wq_ref[...])
    qscale = float(A_NOPE + A_ROPE) ** -0.5
    for h in range(A_HEADS):
        lo = h * A_QBLK
        q_ref[:, lo:lo + A_NOPE] = (q[:, lo:lo + A_NOPE] * qscale).astype(BF16)
        pr = q[:, lo + A_NOPE:lo + A_QBLK] * cs
        pr = pr + pltpu.roll(pr, A_ROPE, 1)
        q_ref[:, lo + A_NOPE:lo + A_QBLK] = (pr * qscale).astype(BF16)
    ckv = (_rms(z[:, A_QLORA:A_QLORA + A_KVLORA]) * gkv_ref[...]).astype(BF16)
    kv = _dot(ckv, wkv_ref[...])
    kn_ref[...] = kv[:, :A_WIDTH].astype(BF16)
    v_ref[...] = kv[:, A_WIDTH:].astype(BF16)
    o = A_QLORA + A_KVLORA
    pk = z[:, o:o + 2 * A_ROPE] * cs
    pk = pk + pltpu.roll(pk, A_ROPE, 1)
    lane = lax.broadcasted_iota(jnp.int32, pk.shape, 1)
    kr_ref[...] = jnp.where(lane < A_ROPE, pk, 0.0).astype(BF16)


def _latent(z, g_q, w_uq, g_kv, w_ukv, cs_tab, seq, tm=512):
    t = z.shape[0]
    per_b = seq // tm
    const = lambda m: (0, 0)
    return pl.pallas_call(
        _latent_kernel,
        grid=(t // tm,),
        in_specs=[
            pl.BlockSpec((tm, LATENT_W), lambda m: (m, 0)),
            pl.BlockSpec((1, A_QLORA), const),
            pl.BlockSpec(w_uq.shape, const),
            pl.BlockSpec((1, A_KVLORA), const),
            pl.BlockSpec(w_ukv.shape, const),
            pl.BlockSpec((tm, 2 * A_ROPE), lambda m: (m % per_b, 0)),
        ],
        out_specs=[
            pl.BlockSpec((tm, A_HEADS * A_QBLK), lambda m: (m, 0)),
            pl.BlockSpec((tm, A_WIDTH), lambda m: (m, 0)),
            pl.BlockSpec((tm, A_WIDTH), lambda m: (m, 0)),
            pl.BlockSpec((tm, 2 * A_ROPE), lambda m: (m, 0)),
        ],
        out_shape=[
            jax.ShapeDtypeStruct((t, A_HEADS * A_QBLK), BF16),
            jax.ShapeDtypeStruct((t, A_WIDTH), BF16),
            jax.ShapeDtypeStruct((t, A_WIDTH), BF16),
            jax.ShapeDtypeStruct((t, 2 * A_ROPE), BF16),
        ],
        compiler_params=_params("arbitrary"),
        name="latent",
    )(z, g_q, w_uq, g_kv, w_ukv, cs_tab)


def _softmax_pv(s_parts, v_parts):
    m = functools.reduce(jnp.maximum, [jnp.max(s, axis=-1, keepdims=True) for s in s_parts])
    ps = [jnp.exp(s - m) for s in s_parts]
    den = functools.reduce(jnp.add, [jnp.sum(p, axis=-1, keepdims=True) for p in ps])
    acc = functools.reduce(jnp.add, [_dot(p.astype(BF16), v) for p, v in zip(ps, v_parts)])
    return acc / den


def _chunk_causal_mask(n):
    r = lax.broadcasted_iota(jnp.int32, (n, n), 0) // CHUNK
    c = lax.broadcasted_iota(jnp.int32, (n, n), 1) // CHUNK
    return c <= r


def _mla_attn_kernel(q_ref, kn_ref, kr_ref, v_ref, o_ref):
    seq = q_ref.shape[0]
    k = jnp.concatenate([kn_ref[...], kr_ref[...]], axis=1)
    mask = _chunk_causal_mask(ATT_TQ)
    for i in range(seq // ATT_TQ):
        lo = i * ATT_TQ
        q = q_ref[lo:lo + ATT_TQ, :]
        s_diag = jnp.where(mask, _dot_nt(q, k[lo:lo + ATT_TQ]), -jnp.inf)
        s_parts, v_parts = [s_diag], [v_ref[lo:lo + ATT_TQ, :]]
        if i > 0:
            s_parts.append(_dot_nt(q, k[:lo]))
            v_parts.append(v_ref[:lo, :])
        o_ref[lo:lo + ATT_TQ, :] = _softmax_pv(s_parts, v_parts).astype(BF16)


def _mla_attention(q, kn, kr, v):
    bsz, seq, _ = q.shape
    head = lambda b, h: (b, 0, h)
    return pl.pallas_call(
        _mla_attn_kernel,
        grid=(bsz, A_HEADS),
        in_specs=[
            pl.BlockSpec((None, seq, A_QBLK), head),
            pl.BlockSpec((None, seq, A_NOPE), head),
            pl.BlockSpec((None, seq, 2 * A_ROPE), lambda b, h: (b, 0, 0)),
            pl.BlockSpec((None, seq, A_VDIM), head),
        ],
        out_specs=pl.BlockSpec((None, seq, A_VDIM), head),
        out_shape=jax.ShapeDtypeStruct((bsz, seq, A_WIDTH), BF16),
        compiler_params=_params("arbitrary", "arbitrary"),
        name="mla_attention",
    )(q, kn, kr, v)


def _band_attn_kernel(q_ref, k_ref, v_ref, bias_ref, o_ref):
    seq = q_ref.shape[0]
    scale = float(B_HDIM) ** -0.5
    for i in range(seq // ATT_TQ):
        lo = i * ATT_TQ
        klo = max(0, lo - B_PREV * CHUNK)
        nk = lo + ATT_TQ - klo
        q = (q_ref[lo:lo + ATT_TQ, :].astype(F32) * scale).astype(BF16)
        s = _dot_nt(q, k_ref[klo:klo + nk, :]) + bias_ref[:, BAND_KEYS - nk:]
        o_ref[lo:lo + ATT_TQ, :] = _softmax_pv([s], [v_ref[klo:klo + nk, :]]).astype(BF16)


def _band_attention(z3, bias):
    bsz, seq, _ = z3.shape
    nb = LATENT_W // B_HDIM

    def col(off):
        return pl.BlockSpec((None, seq, B_HDIM), lambda b, h: (b, 0, off + h))

    return pl.pallas_call(
        _band_attn_kernel,
        grid=(bsz, B_HEADS),
        in_specs=[
            col(nb), col(nb + B_HEADS), col(nb + 2 * B_HEADS),
            pl.BlockSpec((None, ATT_TQ, BAND_KEYS), lambda b, h: (h, 0, 0)),
        ],
        out_specs=pl.BlockSpec((None, seq, B_HDIM), lambda b, h: (b, 0, h)),
        out_shape=jax.ShapeDtypeStruct((bsz, seq, B_WIDTH), BF16),
        compiler_params=_params("arbitrary", "arbitrary"),
        name="band_attention",
    )(z3, z3, z3, bias)


def _band_bias(rel_table):
    r = jnp.arange(ATT_TQ)[:, None]
    c = jnp.arange(BAND_KEYS)[None, :]
    kpos = c - B_PREV * CHUNK
    qc = r // CHUNK
    kc = c // CHUNK - B_PREV
    valid = (kc <= qc) & (kc >= qc - B_PREV)
    idx = jnp.clip(r - kpos, -REL_CLIP, REL_CLIP) + REL_CLIP
    return jnp.where(valid[None], rel_table[:, idx].astype(F32), -jnp.inf)


def _finish(y, x_ref, w_ref, gpost_ref, mgate_ref, o_ref):
    r = _dot(y.astype(BF16), w_ref[...])
    o_ref[...] = x_ref[...] + mgate_ref[...] * (_rms(r) * gpost_ref[...])


def _outproj_even_kernel(oa_ref, ob_ref, gate_ref, x_ref, w_ref, gpost_ref, mgate_ref, o_ref):
    o = jnp.concatenate([oa_ref[...], ob_ref[...]], axis=1).astype(F32)
    y = o * _silu(gate_ref[...].astype(F32))
    _finish(y, x_ref, w_ref, gpost_ref, mgate_ref, o_ref)


def _outproj_even(oa, ob, z, x2, w, g_post, mgate, seq, tm=256):
    t, d = x2.shape
    per_b = seq // tm
    row = lambda m: (m, 0)
    return pl.pallas_call(
        _outproj_even_kernel,
        grid=(t // tm,),
        in_specs=[
            pl.BlockSpec((tm, A_WIDTH), row),
            pl.BlockSpec((tm, B_WIDTH), row),
            pl.BlockSpec((tm, d), lambda m: (m, 2)),
            pl.BlockSpec((tm, d), row),
            pl.BlockSpec((d, d), lambda m: (0, 0)),
            pl.BlockSpec((1, d), lambda m: (0, 0)),
            pl.BlockSpec((None, 1, d), lambda m: (m // per_b, 0, 0)),
        ],
        out_specs=pl.BlockSpec((tm, d), row),
        out_shape=jax.ShapeDtypeStruct((t, d), F32),
        compiler_params=_params("arbitrary"),
        name="outproj_even",
    )(oa, ob, z, x2, w, g_post, mgate)


def _sgu_kernel(u_ref, v_ref, gate_ref, x_ref, lng_ref, lnb_ref, ws_ref, bs_ref,
                w_ref, gpost_ref, mgate_ref, o_ref, sv_ref):
    tm, d = x_ref.shape
    dg = d // SG_GROUPS
    v = v_ref[...].astype(F32)
    mu = jnp.mean(v, axis=-1, keepdims=True)
    vc = v - mu
    var = jnp.mean(vc * vc, axis=-1, keepdims=True)
    vn = ((vc * lax.rsqrt(var + EPS)) * lng_ref[...] + lnb_ref[...]).astype(BF16)
    cpos_r = lax.broadcasted_iota(jnp.int32, (SG_LEN, SG_LEN), 0) // CHUNK
    cpos_c = lax.broadcasted_iota(jnp.int32, (SG_LEN, SG_LEN), 1) // CHUNK
    mask = cpos_c <= cpos_r
    for g in range(SG_GROUPS):
        ws = jnp.where(mask, ws_ref[g], 0.0).astype(BF16)
        bs = bs_ref[:, g:g + 1]
        for n in range(tm // SG_LEN):
            blk = vn[n * SG_LEN:(n + 1) * SG_LEN, g * dg:(g + 1) * dg]
            sv_ref[n * SG_LEN:(n + 1) * SG_LEN, g * dg:(g + 1) * dg] = _dot(ws, blk) + bs
    y = u_ref[...].astype(F32) * sv_ref[...] * _silu(gate_ref[...].astype(F32))
    _finish(y, x_ref, w_ref, gpost_ref, mgate_ref, o_ref)


def _sgu_outproj(z, x2, ln_g, ln_b, w_s, b_s_t, w, g_post, mgate, seq, tm=256):
    t, d = x2.shape
    per_b = seq // tm
    row = lambda m: (m, 0)
    const2 = lambda m: (0, 0)
    return pl.pallas_call(
        _sgu_kernel,
        grid=(t // tm,),
        in_specs=[
            pl.BlockSpec((tm, d), lambda m: (m, 0)),
            pl.BlockSpec((tm, d), lambda m: (m, 1)),
            pl.BlockSpec((tm, d), lambda m: (m, 2)),
            pl.BlockSpec((tm, d), row),
            pl.BlockSpec((1, d), const2),
            pl.BlockSpec((1, d), const2),
            pl.BlockSpec(w_s.shape, lambda m: (0, 0, 0)),
            pl.BlockSpec(b_s_t.shape, const2),
            pl.BlockSpec((d, d), const2),
            pl.BlockSpec((1, d), const2),
            pl.BlockSpec((None, 1, d), lambda m: (m // per_b, 0, 0)),
        ],
        out_specs=pl.BlockSpec((tm, d), row),
        out_shape=jax.ShapeDtypeStruct((t, d), F32),
        scratch_shapes=[pltpu.VMEM((tm, d), F32)],
        compiler_params=_params("arbitrary"),
        name="sgu_outproj",
    )(z, z, z, x2, ln_g, ln_b, w_s, b_s_t, w, g_post, mgate)


def _half_swap(w):
    half = w.shape[-1] // 2
    return jnp.concatenate([-w[..., half:], w[..., :half]], axis=-1)


def _prep_even_weights(w_in, w_uq, w_ukv):
    d = w_in.shape[0]
    o_kr = A_QLORA + A_KVLORA
    o_b = o_kr + A_ROPE
    w_kr = w_in[:, o_kr:o_b]
    pad = jnp.zeros((d, LATENT_W - o_b - A_ROPE), w_in.dtype)
    w_in_p = jnp.concatenate([w_in[:, :o_b], _half_swap(w_kr), pad, w_in[:, o_b:]], axis=1)
    wq = w_uq.reshape(A_QLORA, A_HEADS, A_NOPE + A_ROPE)
    wq_rope = wq[..., A_NOPE:]
    wq_p = jnp.concatenate([wq, _half_swap(wq_rope)], axis=-1).reshape(A_QLORA, A_HEADS * A_QBLK)
    wkv = w_ukv.reshape(A_KVLORA, A_HEADS, A_NOPE + A_VDIM)
    wkv_p = jnp.concatenate([wkv[..., :A_NOPE].reshape(A_KVLORA, A_WIDTH),
                             wkv[..., A_NOPE:].reshape(A_KVLORA, A_WIDTH)], axis=1)
    return w_in_p.astype(BF16), wq_p.astype(BF16), wkv_p.astype(BF16)


def _rope_table(seq):
    half = A_ROPE // 2
    pos = jnp.arange(seq, dtype=F32)
    freqs = ROPE_THETA ** (-jnp.arange(half, dtype=F32) / half)
    ang = pos[:, None] * freqs[None, :]
    cos, sin = jnp.cos(ang), jnp.sin(ang)
    return jnp.concatenate([cos, cos, sin, sin], axis=1)


def kernel(x, c, w_mod, b_mod, g_pre, g_post, ab_w_in, a_g_q, a_w_uq, a_g_kv, a_w_ukv,
           b_rel_bias, ab_w_out, sg_w_in, sg_ln_g, sg_ln_b, sg_w_s, sg_b_s, sg_w_out):
    bsz, seq, d = x.shape
    t = bsz * seq
    mod = _modulation(c, w_mod, b_mod)
    cs_tab = _rope_table(seq)
    x2 = x.reshape(t, d)
    for l in range(DEPTH):
        shift = mod[l, :, None, :d]
        scale = mod[l, :, None, d:2 * d]
        mgate = mod[l, :, None, 2 * d:]
        i = l // 2
        if l % 2 == 0:
            w_in, w_uq, w_ukv = _prep_even_weights(ab_w_in[i], a_w_uq[i], a_w_ukv[i])
            z = _inproj(x2, g_pre[l][None], scale, shift, w_in, seq)
            q, kn, v, kr = _latent(z, a_g_q[i][None], w_uq, a_g_kv[i][None], w_ukv, cs_tab, seq)
            oa = _mla_attention(q.reshape(bsz, seq, -1), kn.reshape(bsz, seq, -1),
                                kr.reshape(bsz, seq, -1), v.reshape(bsz, seq, -1))
            ob = _band_attention(z.reshape(bsz, seq, Z_W), _band_bias(b_rel_bias[i]))
            x2 = _outproj_even(oa.reshape(t, A_WIDTH), ob.reshape(t, B_WIDTH), z, x2,
                               ab_w_out[i].astype(BF16), g_post[l][None], mgate, seq)
        else:
            z = _inproj(x2, g_pre[l][None], scale, shift, sg_w_in[i].astype(BF16), seq)
            x2 = _sgu_outproj(z, x2, sg_ln_g[i][None], sg_ln_b[i][None], sg_w_s[i],
                              sg_b_s[i].T, sg_w_out[i].astype(BF16), g_post[l][None], mgate, seq)
    return x2.reshape(bsz, seq, d)
```

```python
import functools

import jax
import jax.numpy as jnp
from jax import lax
from jax.experimental import pallas as pl
from jax.experimental.pallas import tpu as pltpu

D_MODEL = 2048
DEPTH = 4
CHUNK = 64
EPS = 1e-6
A_HEADS = 8
A_NOPE = 128
A_ROPE = 64
A_VDIM = 128
A_QLORA = 512
A_KVLORA = 256
ROPE_THETA = 10000.0
B_HEADS = 8
B_HDIM = 128
B_PREV = 8
REL_CLIP = 128
SG_GROUPS = 8
SG_LEN = 128
B_WIDTH = B_HEADS * B_HDIM
A_WIDTH = A_HEADS * A_VDIM

VMEM_LIMIT_BYTES = 56 * 1024 * 1024

LATENT_W = 1024
Z_W = 6144
A_QBLK = 256
ATT_TQ = 256
BAND_KEYS = ATT_TQ + B_PREV * CHUNK

BF16 = jnp.bfloat16
F32 = jnp.float32


def _params(*sem):
    return pltpu.CompilerParams(dimension_semantics=sem, vmem_limit_bytes=VMEM_LIMIT_BYTES)


def _dot(a, b):
    return jnp.dot(a, b, preferred_element_type=F32)


def _dot_nt(a, b):
    return lax.dot_general(a, b, (((1,), (1,)), ((), ())), preferred_element_type=F32)


def _rms(x):
    return x * lax.rsqrt(jnp.mean(x * x, axis=-1, keepdims=True) + EPS)


def _silu(x):
    return x * jax.nn.sigmoid(x)


def _mod_kernel(c_ref, w_ref, b_ref, o_ref):
    cs = _silu(c_ref[...]).astype(BF16)
    o_ref[...] = _dot(cs, w_ref[...].astype(BF16)) + b_ref[...]


def _modulation(c, w_mod, b_mod, tn=512):
    depth, d, n = w_mod.shape
    bsz = c.shape[0]
    return pl.pallas_call(
        _mod_kernel,
        grid=(depth, n // tn),
        in_specs=[
            pl.BlockSpec((bsz, d), lambda l, j: (0, 0)),
            pl.BlockSpec((None, d, tn), lambda l, j: (l, 0, j)),
            pl.BlockSpec((None, 1, tn), lambda l, j: (l, 0, j)),
        ],
        out_specs=pl.BlockSpec((None, bsz, tn), lambda l, j: (l, 0, j)),
        out_shape=jax.ShapeDtypeStruct((depth, bsz, n), F32),
        compiler_params=_params("arbitrary", "arbitrary"),
        name="modulation",
    )(c, w_mod, b_mod.reshape(depth, 1, n))


def _inproj_kernel(x_ref, g_ref, scale_ref, shift_ref, w_ref, o_ref, h_ref):
    @pl.when(pl.program_id(1) == 0)
    def _():
        h = _rms(x_ref[...]) * g_ref[...] * (1.0 + scale_ref[...]) + shift_ref[...]
        h_ref[...] = h.astype(BF16)

    o_ref[...] = _dot(h_ref[...], w_ref[...]).astype(BF16)


def _inproj(x2, g_pre, scale, shift, w, seq, tm=1024, tn=1024):
    t, d = x2.shape
    n = w.shape[1]
    per_b = seq // tm
    vec = pl.BlockSpec((None, 1, d), lambda m, j: (m // per_b, 0, 0))
    return pl.pallas_call(
        _inproj_kernel,
        grid=(t // tm, n // tn),
        in_specs=[
            pl.BlockSpec((tm, d), lambda m, j: (m, 0)),
            pl.BlockSpec((1, d), lambda m, j: (0, 0)),
            vec, vec,
            pl.BlockSpec((d, tn), lambda m, j: (0, j)),
        ],
        out_specs=pl.BlockSpec((tm, tn), lambda m, j: (m, j)),
        out_shape=jax.ShapeDtypeStruct((t, n), BF16),
        scratch_shapes=[pltpu.VMEM((tm, d), BF16)],
        compiler_params=_params("arbitrary", "arbitrary"),
        name="inproj",
    )(x2, g_pre, scale, shift, w)


def _latent_kernel(z_ref, gq_ref, wq_ref, gkv_ref, wkv_ref, cs_ref,
                   q_ref, kn_ref, v_ref, kr_ref):
    z = z_ref[...].astype(F32)
    cs = cs_ref[...]
    cq = (_rms(z[:, :A_QLORA]) * gq_ref[...]).astype(BF16)
    q = _dot(cq, wq_ref[...])
    qscale = float(A_NOPE + A_ROPE) ** -0.5
    for h in range(A_HEADS):
        lo = h * A_QBLK
        q_ref[:, lo:lo + A_NOPE] = (q[:, lo:lo + A_NOPE] * qscale).astype(BF16)
        pr = q[:, lo + A_NOPE:lo + A_QBLK] * cs
        pr = pr + pltpu.roll(pr, A_ROPE, 1)
        q_ref[:, lo + A_NOPE:lo + A_QBLK] = (pr * qscale).astype(BF16)
    ckv = (_rms(z[:, A_QLORA:A_QLORA + A_KVLORA]) * gkv_ref[...]).astype(BF16)
    kv = _dot(ckv, wkv_ref[...])
    kn_ref[...] = kv[:, :A_WIDTH].astype(BF16)
    v_ref[...] = kv[:, A_WIDTH:].astype(BF16)
    o = A_QLORA + A_KVLORA
    pk = z[:, o:o + 2 * A_ROPE] * cs
    pk = pk + pltpu.roll(pk, A_ROPE, 1)
    lane = lax.broadcasted_iota(jnp.int32, pk.shape, 1)
    kr_ref[...] = jnp.where(lane < A_ROPE, pk, 0.0).astype(BF16)


def _latent(z, g_q, w_uq, g_kv, w_ukv, cs_tab, seq, tm=512):
    t = z.shape[0]
    per_b = seq // tm
    const = lambda m: (0, 0)
    return pl.pallas_call(
        _latent_kernel,
        grid=(t // tm,),
        in_specs=[
            pl.BlockSpec((tm, LATENT_W), lambda m: (m, 0)),
            pl.BlockSpec((1, A_QLORA), const),
            pl.BlockSpec(w_uq.shape, const),
            pl.BlockSpec((1, A_KVLORA), const),
            pl.BlockSpec(w_ukv.shape, const),
            pl.BlockSpec((tm, 2 * A_ROPE), lambda m: (m % per_b, 0)),
        ],
        out_specs=[
            pl.BlockSpec((tm, A_HEADS * A_QBLK), lambda m: (m, 0)),
            pl.BlockSpec((tm, A_WIDTH), lambda m: (m, 0)),
            pl.BlockSpec((tm, A_WIDTH), lambda m: (m, 0)),
            pl.BlockSpec((tm, 2 * A_ROPE), lambda m: (m, 0)),
        ],
        out_shape=[
            jax.ShapeDtypeStruct((t, A_HEADS * A_QBLK), BF16),
            jax.ShapeDtypeStruct((t, A_WIDTH), BF16),
            jax.ShapeDtypeStruct((t, A_WIDTH), BF16),
            jax.ShapeDtypeStruct((t, 2 * A_ROPE), BF16),
        ],
        compiler_params=_params("arbitrary"),
        name="latent",
    )(z, g_q, w_uq, g_kv, w_ukv, cs_tab)


def _softmax_pv(s_parts, v_parts):
    m = functools.reduce(jnp.maximum, [jnp.max(s, axis=-1, keepdims=True) for s in s_parts])
    ps = [jnp.exp(s - m) for s in s_parts]
    den = functools.reduce(jnp.add, [jnp.sum(p, axis=-1, keepdims=True) for p in ps])
    acc = functools.reduce(jnp.add, [_dot(p.astype(BF16), v) for p, v in zip(ps, v_parts)])
    return acc / den


def _chunk_causal_mask(n):
    r = lax.broadcasted_iota(jnp.int32, (n, n), 0) // CHUNK
    c = lax.broadcasted_iota(jnp.int32, (n, n), 1) // CHUNK
    return c <= r


def _mla_attn_kernel(q_ref, kn_ref, kr_ref, v_ref, o_ref):
    seq = q_ref.shape[0]
    k = jnp.concatenate([kn_ref[...], kr_ref[...]], axis=1)
    mask = _chunk_causal_mask(ATT_TQ)
    for i in range(seq // ATT_TQ):
        lo = i * ATT_TQ
        q = q_ref[lo:lo + ATT_TQ, :]
        s_diag = jnp.where(mask, _dot_nt(q, k[lo:lo + ATT_TQ]), -jnp.inf)
        s_parts, v_parts = [s_diag], [v_ref[lo:lo + ATT_TQ, :]]
        if i > 0:
            s_parts.append(_dot_nt(q, k[:lo]))
            v_parts.append(v_ref[:lo, :])
        o_ref[lo:lo + ATT_TQ, :] = _softmax_pv(s_parts, v_parts).astype(BF16)


def _mla_attention(q, kn, kr, v):
    bsz, seq, _ = q.shape
    head = lambda b, h: (b, 0, h)
    return pl.pallas_call(
        _mla_attn_kernel,
        grid=(bsz, A_HEADS),
        in_specs=[
            pl.BlockSpec((None, seq, A_QBLK), head),
            pl.BlockSpec((None, seq, A_NOPE), head),
            pl.BlockSpec((None, seq, 2 * A_ROPE), lambda b, h: (b, 0, 0)),
            pl.BlockSpec((None, seq, A_VDIM), head),
        ],
        out_specs=pl.BlockSpec((None, seq, A_VDIM), head),
        out_shape=jax.ShapeDtypeStruct((bsz, seq, A_WIDTH), BF16),
        compiler_params=_params("arbitrary", "arbitrary"),
        name="mla_attention",
    )(q, kn, kr, v)


def _band_attn_kernel(q_ref, k_ref, v_ref, bias_ref, o_ref):
    seq = q_ref.shape[0]
    scale = float(B_HDIM) ** -0.5
    for i in range(seq // ATT_TQ):
        lo = i * ATT_TQ
        klo = max(0, lo - B_PREV * CHUNK)
        nk = lo + ATT_TQ - klo
        q = (q_ref[lo:lo + ATT_TQ, :].astype(F32) * scale).astype(BF16)
        s = _dot_nt(q, k_ref[klo:klo + nk, :]) + bias_ref[:, BAND_KEYS - nk:]
        o_ref[lo:lo + ATT_TQ, :] = _softmax_pv([s], [v_ref[klo:klo + nk, :]]).astype(BF16)


def _band_attention(z3, bias):
    bsz, seq, _ = z3.shape
    nb = LATENT_W // B_HDIM

    def col(off):
        return pl.BlockSpec((None, seq, B_HDIM), lambda b, h: (b, 0, off + h))

    return pl.pallas_call(
        _band_attn_kernel,
        grid=(bsz, B_HEADS),
        in_specs=[
            col(nb), col(nb + B_HEADS), col(nb + 2 * B_HEADS),
            pl.BlockSpec((None, ATT_TQ, BAND_KEYS), lambda b, h: (h, 0, 0)),
        ],
        out_specs=pl.BlockSpec((None, seq, B_HDIM), lambda b, h: (b, 0, h)),
        out_shape=jax.ShapeDtypeStruct((bsz, seq, B_WIDTH), BF16),
        compiler_params=_params("arbitrary", "arbitrary"),
        name="band_attention",
    )(z3, z3, z3, bias)


def _band_bias(rel_table):
    h = rel_table.shape[0]
    prev = B_PREV * CHUNK
    n_hi = prev - REL_CLIP
    n_lo = (BAND_KEYS + 1) - n_hi - (2 * REL_CLIP + 1) - (CHUNK - 1)
    hi = rel_table[:, 2 * REL_CLIP:]
    lo = rel_table[:, :1]
    e = jnp.concatenate([jnp.broadcast_to(hi, (h, n_hi)), rel_table[:, ::-1],
                         jnp.broadcast_to(lo, (h, n_lo)),
                         jnp.broadcast_to(hi, (h, CHUNK - 1))], axis=1)
    toep = jnp.tile(e, (1, ATT_TQ))[:, :ATT_TQ * BAND_KEYS].reshape(h, ATT_TQ, BAND_KEYS)
    qc = jnp.arange(ATT_TQ)[:, None] // CHUNK
    kc = jnp.arange(BAND_KEYS)[None, :] // CHUNK - B_PREV
    valid = (kc <= qc) & (kc >= qc - B_PREV)
    return jnp.where(valid[None], toep.astype(F32), -jnp.inf)


def _finish(y, x_ref, w_ref, gpost_ref, mgate_ref, o_ref):
    r = _dot(y.astype(BF16), w_ref[...])
    o_ref[...] = x_ref[...] + mgate_ref[...] * (_rms(r) * gpost_ref[...])


def _outproj_even_kernel(oa_ref, ob_ref, gate_ref, x_ref, w_ref, gpost_ref, mgate_ref, o_ref):
    o = jnp.concatenate([oa_ref[...], ob_ref[...]], axis=1).astype(F32)
    y = o * _silu(gate_ref[...].astype(F32))
    _finish(y, x_ref, w_ref, gpost_ref, mgate_ref, o_ref)


def _outproj_even(oa, ob, z, x2, w, g_post, mgate, seq, tm=256):
    t, d = x2.shape
    per_b = seq // tm
    row = lambda m: (m, 0)
    return pl.pallas_call(
        _outproj_even_kernel,
        grid=(t // tm,),
        in_specs=[
            pl.BlockSpec((tm, A_WIDTH), row),
            pl.BlockSpec((tm, B_WIDTH), row),
            pl.BlockSpec((tm, d), lambda m: (m, 2)),
            pl.BlockSpec((tm, d), row),
            pl.BlockSpec((d, d), lambda m: (0, 0)),
            pl.BlockSpec((1, d), lambda m: (0, 0)),
            pl.BlockSpec((None, 1, d), lambda m: (m // per_b, 0, 0)),
        ],
        out_specs=pl.BlockSpec((tm, d), row),
        out_shape=jax.ShapeDtypeStruct((t, d), F32),
        compiler_params=_params("arbitrary"),
        name="outproj_even",
    )(oa, ob, z, x2, w, g_post, mgate)


def _sgu_kernel(u_ref, v_ref, gate_ref, x_ref, lng_ref, lnb_ref, ws_ref, bs_ref,
                w_ref, gpost_ref, mgate_ref, o_ref, sv_ref):
    tm, d = x_ref.shape
    dg = d // SG_GROUPS
    v = v_ref[...].astype(F32)
    mu = jnp.mean(v, axis=-1, keepdims=True)
    vc = v - mu
    var = jnp.mean(vc * vc, axis=-1, keepdims=True)
    vn = ((vc * lax.rsqrt(var + EPS)) * lng_ref[...] + lnb_ref[...]).astype(BF16)
    cpos_r = lax.broadcasted_iota(jnp.int32, (SG_LEN, SG_LEN), 0) // CHUNK
    cpos_c = lax.broadcasted_iota(jnp.int32, (SG_LEN, SG_LEN), 1) // CHUNK
    mask = cpos_c <= cpos_r
    for g in range(SG_GROUPS):
        ws = jnp.where(mask, ws_ref[g], 0.0).astype(BF16)
        bs = bs_ref[:, g:g + 1]
        for n in range(tm // SG_LEN):
            blk = vn[n * SG_LEN:(n + 1) * SG_LEN, g * dg:(g + 1) * dg]
            sv_ref[n * SG_LEN:(n + 1) * SG_LEN, g * dg:(g + 1) * dg] = _dot(ws, blk) + bs
    y = u_ref[...].astype(F32) * sv_ref[...] * _silu(gate_ref[...].astype(F32))
    _finish(y, x_ref, w_ref, gpost_ref, mgate_ref, o_ref)


def _sgu_outproj(z, x2, ln_g, ln_b, w_s, b_s_t, w, g_post, mgate, seq, tm=256):
    t, d = x2.shape
    per_b = seq // tm
    row = lambda m: (m, 0)
    const2 = lambda m: (0, 0)
    return pl.pallas_call(
        _sgu_kernel,
        grid=(t // tm,),
        in_specs=[
            pl.BlockSpec((tm, d), lambda m: (m, 0)),
            pl.BlockSpec((tm, d), lambda m: (m, 1)),
            pl.BlockSpec((tm, d), lambda m: (m, 2)),
            pl.BlockSpec((tm, d), row),
            pl.BlockSpec((1, d), const2),
            pl.BlockSpec((1, d), const2),
            pl.BlockSpec(w_s.shape, lambda m: (0, 0, 0)),
            pl.BlockSpec(b_s_t.shape, const2),
            pl.BlockSpec((d, d), const2),
            pl.BlockSpec((1, d), const2),
            pl.BlockSpec((None, 1, d), lambda m: (m // per_b, 0, 0)),
        ],
        out_specs=pl.BlockSpec((tm, d), row),
        out_shape=jax.ShapeDtypeStruct((t, d), F32),
        scratch_shapes=[pltpu.VMEM((tm, d), F32)],
        compiler_params=_params("arbitrary"),
        name="sgu_outproj",
    )(z, z, z, x2, ln_g, ln_b, w_s, b_s_t, w, g_post, mgate)


def _half_swap(w):
    half = w.shape[-1] // 2
    return jnp.concatenate([-w[..., half:], w[..., :half]], axis=-1)


def _prep_even_weights(w_in, w_uq, w_ukv):
    d = w_in.shape[0]
    o_kr = A_QLORA + A_KVLORA
    o_b = o_kr + A_ROPE
    w_kr = w_in[:, o_kr:o_b]
    pad = jnp.zeros((d, LATENT_W - o_b - A_ROPE), w_in.dtype)
    w_in_p = jnp.concatenate([w_in[:, :o_b], _half_swap(w_kr), pad, w_in[:, o_b:]], axis=1)
    wq = w_uq.reshape(A_QLORA, A_HEADS, A_NOPE + A_ROPE)
    wq_rope = wq[..., A_NOPE:]
    wq_p = jnp.concatenate([wq, _half_swap(wq_rope)], axis=-1).reshape(A_QLORA, A_HEADS * A_QBLK)
    wkv = w_ukv.reshape(A_KVLORA, A_HEADS, A_NOPE + A_VDIM)
    wkv_p = jnp.concatenate([wkv[..., :A_NOPE].reshape(A_KVLORA, A_WIDTH),
                             wkv[..., A_NOPE:].reshape(A_KVLORA, A_WIDTH)], axis=1)
    return w_in_p.astype(BF16), wq_p.astype(BF16), wkv_p.astype(BF16)


def _rope_table(seq):
    half = A_ROPE // 2
    pos = jnp.arange(seq, dtype=F32)
    freqs = ROPE_THETA ** (-jnp.arange(half, dtype=F32) / half)
    ang = pos[:, None] * freqs[None, :]
    cos, sin = jnp.cos(ang), jnp.sin(ang)
    return jnp.concatenate([cos, cos, sin, sin], axis=1)


def kernel(x, c, w_mod, b_mod, g_pre, g_post, ab_w_in, a_g_q, a_w_uq, a_g_kv, a_w_ukv,
           b_rel_bias, ab_w_out, sg_w_in, sg_ln_g, sg_ln_b, sg_w_s, sg_b_s, sg_w_out):
    bsz, seq, d = x.shape
    t = bsz * seq
    mod = _modulation(c, w_mod, b_mod)
    cs_tab = _rope_table(seq)
    x2 = x.reshape(t, d)
    for l in range(DEPTH):
        shift = mod[l, :, None, :d]
        scale = mod[l, :, None, d:2 * d]
        mgate = mod[l, :, None, 2 * d:]
        i = l // 2
        if l % 2 == 0:
            w_in, w_uq, w_ukv = _prep_even_weights(ab_w_in[i], a_w_uq[i], a_w_ukv[i])
            z = _inproj(x2, g_pre[l][None], scale, shift, w_in, seq)
            q, kn, v, kr = _latent(z, a_g_q[i][None], w_uq, a_g_kv[i][None], w_ukv, cs_tab, seq)
            oa = _mla_attention(q.reshape(bsz, seq, -1), kn.reshape(bsz, seq, -1),
                                kr.reshape(bsz, seq, -1), v.reshape(bsz, seq, -1))
            ob = _band_attention(z.reshape(bsz, seq, Z_W), _band_bias(b_rel_bias[i]))
            x2 = _outproj_even(oa.reshape(t, A_WIDTH), ob.reshape(t, B_WIDTH), z, x2,
                               ab_w_out[i].astype(BF16), g_post[l][None], mgate, seq)
        else:
            z = _inproj(x2, g_pre[l][None], scale, shift, sg_w_in[i].astype(BF16), seq)
            x2 = _sgu_outproj(z, x2, sg_ln_g[i][None], sg_ln_b[i][None], sg_w_s[i],
                              sg_b_s[i].T, sg_w_out[i].astype(BF16), g_post[l][None], mgate, seq)
    return x2.reshape(bsz, seq, d)
```

```python
import functools

import jax
import jax.numpy as jnp
from jax import lax
from jax.experimental import pallas as pl
from jax.experimental.pallas import tpu as pltpu

D_MODEL = 2048
DEPTH = 4
CHUNK = 64
EPS = 1e-6
A_HEADS = 8
A_NOPE = 128
A_ROPE = 64
A_VDIM = 128
A_QLORA = 512
A_KVLORA = 256
ROPE_THETA = 10000.0
B_HEADS = 8
B_HDIM = 128
B_PREV = 8
REL_CLIP = 128
SG_GROUPS = 8
SG_LEN = 128
B_WIDTH = B_HEADS * B_HDIM
A_WIDTH = A_HEADS * A_VDIM

VMEM_LIMIT_BYTES = 56 * 1024 * 1024

LATENT_W = 1024
Z_W = 6144
A_QBLK = 256
ATT_TQ = 256
BAND_KEYS = ATT_TQ + B_PREV * CHUNK

BF16 = jnp.bfloat16
F32 = jnp.float32


def _params(*sem):
    return pltpu.CompilerParams(dimension_semantics=sem, vmem_limit_bytes=VMEM_LIMIT_BYTES)


def _dot(a, b):
    return jnp.dot(a, b, preferred_element_type=F32)


def _dot_nt(a, b):
    return lax.dot_general(a, b, (((1,), (1,)), ((), ())), preferred_element_type=F32)


def _rms(x):
    return x * lax.rsqrt(jnp.mean(x * x, axis=-1, keepdims=True) + EPS)


def _silu(x):
    return x * jax.nn.sigmoid(x)


def _mod_kernel(c_ref, w_ref, b_ref, o_ref):
    cs = _silu(c_ref[...]).astype(BF16)
    o_ref[...] = _dot(cs, w_ref[...].astype(BF16)) + b_ref[...]


def _modulation(c, w_mod, b_mod, tn=512):
    depth, d, n = w_mod.shape
    bsz = c.shape[0]
    return pl.pallas_call(
        _mod_kernel,
        grid=(depth, n // tn),
        in_specs=[
            pl.BlockSpec((bsz, d), lambda l, j: (0, 0)),
            pl.BlockSpec((None, d, tn), lambda l, j: (l, 0, j)),
            pl.BlockSpec((None, 1, tn), lambda l, j: (l, 0, j)),
        ],
        out_specs=pl.BlockSpec((None, bsz, tn), lambda l, j: (l, 0, j)),
        out_shape=jax.ShapeDtypeStruct((depth, bsz, n), F32),
        compiler_params=_params("arbitrary", "arbitrary"),
        name="modulation",
    )(c, w_mod, b_mod.reshape(depth, 1, n))


def _adaln(x, g_ref, scale_ref, shift_ref):
    return (_rms(x) * g_ref[...] * (1.0 + scale_ref[...]) + shift_ref[...]).astype(BF16)


def _prenorm_kernel(x_ref, g_ref, scale_ref, shift_ref, h_ref):
    h_ref[...] = _adaln(x_ref[...], g_ref, scale_ref, shift_ref)


def _prenorm(x2, g_pre, scale, shift, seq, tm=512):
    t, d = x2.shape
    per_b = seq // tm
    vec = pl.BlockSpec((None, 1, d), lambda m: (m // per_b, 0, 0))
    return pl.pallas_call(
        _prenorm_kernel,
        grid=(t // tm,),
        in_specs=[pl.BlockSpec((tm, d), lambda m: (m, 0)),
                  pl.BlockSpec((1, d), lambda m: (0, 0)), vec, vec],
        out_specs=pl.BlockSpec((tm, d), lambda m: (m, 0)),
        out_shape=jax.ShapeDtypeStruct((t, d), BF16),
        compiler_params=_params("arbitrary"),
        name="prenorm",
    )(x2, g_pre, scale, shift)


def _inproj_kernel(seg_bounds, h_ref, *refs):
    nseg = len(seg_bounds)
    w_refs, o_ref, scratch = refs[:nseg], refs[nseg], list(refs[nseg + 1:])
    n, m = pl.program_id(0), pl.program_id(1)
    for (lo, hi), w_ref in zip(seg_bounds, w_refs):
        wb_ref = scratch.pop(0) if w_ref.dtype != BF16 else None

        def body(w_ref=w_ref, wb_ref=wb_ref):
            if wb_ref is not None:
                @pl.when(m == 0)
                def _():
                    wb_ref[...] = w_ref[...].astype(BF16)
                w_ref = wb_ref
            o_ref[...] = _dot(h_ref[...], w_ref[...]).astype(BF16)

        if nseg == 1:
            body()
        else:
            pl.when((n >= lo) & (n < hi))(body)


def _inproj(h, weights, tm=1024, tn=1024):
    t, d = h.shape
    seg_bounds, in_specs, scratch, lo = [], [pl.BlockSpec((tm, d), lambda n, m: (m, 0))], [], 0
    for w in weights:
        ntile = w.shape[1] // tn
        seg_bounds.append((lo, lo + ntile))
        in_specs.append(pl.BlockSpec(
            (d, tn), lambda n, m, lo=lo, ntile=ntile: (0, jnp.clip(n - lo, 0, ntile - 1))))
        if w.dtype != BF16:
            scratch.append(pltpu.VMEM((d, tn), BF16))
        lo += ntile
    return pl.pallas_call(
        functools.partial(_inproj_kernel, tuple(seg_bounds)),
        grid=(lo, t // tm),
        in_specs=in_specs,
        out_specs=pl.BlockSpec((tm, tn), lambda n, m: (m, n)),
        out_shape=jax.ShapeDtypeStruct((t, lo * tn), BF16),
        scratch_shapes=scratch,
        compiler_params=_params("arbitrary", "arbitrary"),
        name="inproj",
    )(h, *weights)


def _latent_kernel(z_ref, gq_ref, wq_ref, gkv_ref, wkv_ref, cs_ref,
                   q_ref, kn_ref, v_ref, kr_ref):
    z = z_ref[...].astype(F32)
    cs = cs_ref[...]
    cq = (_rms(z[:, :A_QLORA]) * gq_ref[...]).astype(BF16)
    q = _dot(cq, wq_ref[...])
    qscale = float(A_NOPE + A_ROPE) ** -0.5
    for h in range(A_HEADS):
        lo = h * A_QBLK
        q_ref[:, lo:lo + A_NOPE] = (q[:, lo:lo + A_NOPE] * qscale).astype(BF16)
        pr = q[:, lo + A_NOPE:lo + A_QBLK] * cs
        pr = pr + pltpu.roll(pr, A_ROPE, 1)
        q_ref[:, lo + A_NOPE:lo + A_QBLK] = (pr * qscale).astype(BF16)
    ckv = (_rms(z[:, A_QLORA:A_QLORA + A_KVLORA]) * gkv_ref[...]).astype(BF16)
    kv = _dot(ckv, wkv_ref[...])
    kn_ref[...] = kv[:, :A_WIDTH].astype(BF16)
    v_ref[...] = kv[:, A_WIDTH:].astype(BF16)
    o = A_QLORA + A_KVLORA
    pk = z[:, o:o + 2 * A_ROPE] * cs
    pk = pk + pltpu.roll(pk, A_ROPE, 1)
    lane = lax.broadcasted_iota(jnp.int32, pk.shape, 1)
    kr_ref[...] = jnp.where(lane < A_ROPE, pk, 0.0).astype(BF16)


def _latent(z, g_q, w_uq, g_kv, w_ukv, cs_tab, seq, tm=512):
    t = z.shape[0]
    per_b = seq // tm
    const = lambda m: (0, 0)
    return pl.pallas_call(
        _latent_kernel,
        grid=(t // tm,),
        in_specs=[
            pl.BlockSpec((tm, LATENT_W), lambda m: (m, 0)),
            pl.BlockSpec((1, A_QLORA), const),
            pl.BlockSpec(w_uq.shape, const),
            pl.BlockSpec((1, A_KVLORA), const),
            pl.BlockSpec(w_ukv.shape, const),
            pl.BlockSpec((tm, 2 * A_ROPE), lambda m: (m % per_b, 0)),
        ],
        out_specs=[
            pl.BlockSpec((tm, A_HEADS * A_QBLK), lambda m: (m, 0)),
            pl.BlockSpec((tm, A_WIDTH), lambda m: (m, 0)),
            pl.BlockSpec((tm, A_WIDTH), lambda m: (m, 0)),
            pl.BlockSpec((tm, 2 * A_ROPE), lambda m: (m, 0)),
        ],
        out_shape=[
            jax.ShapeDtypeStruct((t, A_HEADS * A_QBLK), BF16),
            jax.ShapeDtypeStruct((t, A_WIDTH), BF16),
            jax.ShapeDtypeStruct((t, A_WIDTH), BF16),
            jax.ShapeDtypeStruct((t, 2 * A_ROPE), BF16),
        ],
        compiler_params=_params("arbitrary"),
        name="latent",
    )(z, g_q, w_uq, g_kv, w_ukv, cs_tab)


def _softmax_pv(s_parts, v_parts):
    m = functools.reduce(jnp.maximum, [jnp.max(s, axis=-1, keepdims=True) for s in s_parts])
    ps = [jnp.exp(s - m) for s in s_parts]
    den = functools.reduce(jnp.add, [jnp.sum(p, axis=-1, keepdims=True) for p in ps])
    acc = functools.reduce(jnp.add, [_dot(p.astype(BF16), v) for p, v in zip(ps, v_parts)])
    return acc / den


def _chunk_causal_mask(n):
    r = lax.broadcasted_iota(jnp.int32, (n, n), 0) // CHUNK
    c = lax.broadcasted_iota(jnp.int32, (n, n), 1) // CHUNK
    return c <= r


def _mla_attn_kernel(q_ref, kn_ref, kr_ref, v_ref, o_ref):
    seq = q_ref.shape[0]
    k = jnp.concatenate([kn_ref[...], kr_ref[...]], axis=1)
    mask = _chunk_causal_mask(ATT_TQ)
    for i in range(seq // ATT_TQ):
        lo = i * ATT_TQ
        q = q_ref[lo:lo + ATT_TQ, :]
        s_diag = jnp.where(mask, _dot_nt(q, k[lo:lo + ATT_TQ]), -jnp.inf)
        s_parts, v_parts = [s_diag], [v_ref[lo:lo + ATT_TQ, :]]
        if i > 0:
            s_parts.append(_dot_nt(q, k[:lo]))
            v_parts.append(v_ref[:lo, :])
        o_ref[lo:lo + ATT_TQ, :] = _softmax_pv(s_parts, v_parts).astype(BF16)


def _mla_attention(q, kn, kr, v):
    bsz, seq, _ = q.shape
    head = lambda b, h: (b, 0, h)
    return pl.pallas_call(
        _mla_attn_kernel,
        grid=(bsz, A_HEADS),
        in_specs=[
            pl.BlockSpec((None, seq, A_QBLK), head),
            pl.BlockSpec((None, seq, A_NOPE), head),
            pl.BlockSpec((None, seq, 2 * A_ROPE), lambda b, h: (b, 0, 0)),
            pl.BlockSpec((None, seq, A_VDIM), head),
        ],
        out_specs=pl.BlockSpec((None, seq, A_VDIM), head),
        out_shape=jax.ShapeDtypeStruct((bsz, seq, A_WIDTH), BF16),
        compiler_params=_params("arbitrary", "arbitrary"),
        name="mla_attention",
    )(q, kn, kr, v)


def _band_attn_kernel(q_ref, k_ref, v_ref, bias_ref, o_ref):
    seq = q_ref.shape[0]
    scale = float(B_HDIM) ** -0.5
    for i in range(seq // ATT_TQ):
        lo = i * ATT_TQ
        klo = max(0, lo - B_PREV * CHUNK)
        nk = lo + ATT_TQ - klo
        q = (q_ref[lo:lo + ATT_TQ, :].astype(F32) * scale).astype(BF16)
        s = _dot_nt(q, k_ref[klo:klo + nk, :]) + bias_ref[:, BAND_KEYS - nk:]
        o_ref[lo:lo + ATT_TQ, :] = _softmax_pv([s], [v_ref[klo:klo + nk, :]]).astype(BF16)


def _band_attention(z3, bias):
    bsz, seq, _ = z3.shape
    nb = LATENT_W // B_HDIM

    def col(off):
        return pl.BlockSpec((None, seq, B_HDIM), lambda b, h: (b, 0, off + h))

    return pl.pallas_call(
        _band_attn_kernel,
        grid=(bsz, B_HEADS),
        in_specs=[
            col(nb), col(nb + B_HEADS), col(nb + 2 * B_HEADS),
            pl.BlockSpec((None, ATT_TQ, BAND_KEYS), lambda b, h: (h, 0, 0)),
        ],
        out_specs=pl.BlockSpec((None, seq, B_HDIM), lambda b, h: (b, 0, h)),
        out_shape=jax.ShapeDtypeStruct((bsz, seq, B_WIDTH), BF16),
        compiler_params=_params("arbitrary", "arbitrary"),
        name="band_attention",
    )(z3, z3, z3, bias)


def _band_bias(rel_table):
    h = rel_table.shape[0]
    prev = B_PREV * CHUNK
    n_hi = prev - REL_CLIP
    n_lo = (BAND_KEYS + 1) - n_hi - (2 * REL_CLIP + 1) - (CHUNK - 1)
    hi = rel_table[:, 2 * REL_CLIP:]
    lo = rel_table[:, :1]
    e = jnp.concatenate([jnp.broadcast_to(hi, (h, n_hi)), rel_table[:, ::-1],
                         jnp.broadcast_to(lo, (h, n_lo)),
                         jnp.broadcast_to(hi, (h, CHUNK - 1))], axis=1)
    toep = jnp.tile(e, (1, ATT_TQ))[:, :ATT_TQ * BAND_KEYS].reshape(h, ATT_TQ, BAND_KEYS)
    qc = jnp.arange(ATT_TQ)[:, None] // CHUNK
    kc = jnp.arange(BAND_KEYS)[None, :] // CHUNK - B_PREV
    valid = (kc <= qc) & (kc >= qc - B_PREV)
    return jnp.where(valid[None], toep.astype(F32), -jnp.inf)


def _finish(y, refs, emit_h):
    x_ref, w_ref, gpost_ref, mgate_ref = refs[:4]
    o_ref, wb_ref = refs[7 if emit_h else 4], refs[-1]

    @pl.when(pl.program_id(0) == 0)
    def _():
        wb_ref[...] = w_ref[...].astype(BF16)

    r = _dot(y.astype(BF16), wb_ref[...])
    xn = x_ref[...] + mgate_ref[...] * (_rms(r) * gpost_ref[...])
    o_ref[...] = xn
    if emit_h:
        refs[8][...] = _adaln(xn, *refs[4:7])


def _finish_specs(x2, w, g_post, mgate, nxt, seq, tm):
    t, d = x2.shape
    per_b = seq // tm
    row = pl.BlockSpec((tm, d), lambda m: (m, 0))
    vec = pl.BlockSpec((1, d), lambda m: (0, 0))
    bvec = pl.BlockSpec((None, 1, d), lambda m: (m // per_b, 0, 0))
    whole = pl.BlockSpec((d, d), lambda m: (0, 0), pipeline_mode=pl.Buffered(1))
    args, in_specs = [x2, w, g_post, mgate], [row, whole, vec, bvec]
    out_specs, out_shape = [row], [jax.ShapeDtypeStruct((t, d), F32)]
    if nxt is not None:
        args += list(nxt)
        in_specs += [vec, bvec, bvec]
        out_specs.append(row)
        out_shape.append(jax.ShapeDtypeStruct((t, d), BF16))
    return args, in_specs, out_specs, out_shape, [pltpu.VMEM((d, d), BF16)]


def _outproj_even_kernel(emit_h, oa_ref, ob_ref, gate_ref, *refs):
    o = jnp.concatenate([oa_ref[...], ob_ref[...]], axis=1).astype(F32)
    y = o * _silu(gate_ref[...].astype(F32))
    _finish(y, refs, emit_h)


def _outproj_even(oa, ob, z, x2, w, g_post, mgate, nxt, seq, tm=256):
    t, d = x2.shape
    row = lambda m: (m, 0)
    args, in_specs, out_specs, out_shape, scratch = _finish_specs(x2, w, g_post, mgate, nxt, seq, tm)
    return pl.pallas_call(
        functools.partial(_outproj_even_kernel, nxt is not None),
        grid=(t // tm,),
        in_specs=[
            pl.BlockSpec((tm, A_WIDTH), row),
            pl.BlockSpec((tm, B_WIDTH), row),
            pl.BlockSpec((tm, d), lambda m: (m, 2)),
        ] + in_specs,
        out_specs=out_specs,
        out_shape=out_shape,
        scratch_shapes=scratch,
        compiler_params=_params("arbitrary"),
        name="outproj_even",
    )(oa, ob, z, *args)


def _sgu_kernel(emit_h, u_ref, v_ref, gate_ref, lng_ref, lnb_ref, ws_ref, bs_ref, *refs):
    sv_ref = refs[-2]
    tm, d = sv_ref.shape
    dg = d // SG_GROUPS
    v = v_ref[...].astype(F32)
    mu = jnp.mean(v, axis=-1, keepdims=True)
    vc = v - mu
    var = jnp.mean(vc * vc, axis=-1, keepdims=True)
    vn = ((vc * lax.rsqrt(var + EPS)) * lng_ref[...] + lnb_ref[...]).astype(BF16)
    cpos_r = lax.broadcasted_iota(jnp.int32, (SG_LEN, SG_LEN), 0) // CHUNK
    cpos_c = lax.broadcasted_iota(jnp.int32, (SG_LEN, SG_LEN), 1) // CHUNK
    mask = cpos_c <= cpos_r
    for g in range(SG_GROUPS):
        ws = jnp.where(mask, ws_ref[g], 0.0).astype(BF16)
        bs = bs_ref[:, g:g + 1]
        for n in range(tm // SG_LEN):
            blk = vn[n * SG_LEN:(n + 1) * SG_LEN, g * dg:(g + 1) * dg]
            sv_ref[n * SG_LEN:(n + 1) * SG_LEN, g * dg:(g + 1) * dg] = _dot(ws, blk) + bs
    y = u_ref[...].astype(F32) * sv_ref[...] * _silu(gate_ref[...].astype(F32))
    _finish(y, refs, emit_h)


def _sgu_outproj(z, x2, ln_g, ln_b, w_s, b_s_t, w, g_post, mgate, nxt, seq, tm=256):
    t, d = x2.shape
    const2 = lambda m: (0, 0)
    args, in_specs, out_specs, out_shape, scratch = _finish_specs(x2, w, g_post, mgate, nxt, seq, tm)
    return pl.pallas_call(
        functools.partial(_sgu_kernel, nxt is not None),
        grid=(t // tm,),
        in_specs=[
            pl.BlockSpec((tm, d), lambda m: (m, 0)),
            pl.BlockSpec((tm, d), lambda m: (m, 1)),
            pl.BlockSpec((tm, d), lambda m: (m, 2)),
            pl.BlockSpec((1, d), const2),
            pl.BlockSpec((1, d), const2),
            pl.BlockSpec(w_s.shape, lambda m: (0, 0, 0)),
            pl.BlockSpec(b_s_t.shape, const2),
        ] + in_specs,
        out_specs=out_specs,
        out_shape=out_shape,
        scratch_shapes=[pltpu.VMEM((tm, d), F32)] + scratch,
        compiler_params=_params("arbitrary"),
        name="sgu_outproj",
    )(z, z, z, ln_g, ln_b, w_s, b_s_t, *args)


def _half_swap(w):
    half = w.shape[-1] // 2
    return jnp.concatenate([-w[..., half:], w[..., :half]], axis=-1)


def _prep_even_weights(w_in, w_uq, w_ukv):
    d = w_in.shape[0]
    o_kr = A_QLORA + A_KVLORA
    o_b = o_kr + A_ROPE
    w_kr = w_in[:, o_kr:o_b]
    pad = jnp.zeros((d, LATENT_W - o_b - A_ROPE), w_in.dtype)
    w_lat = jnp.concatenate([w_in[:, :o_b], _half_swap(w_kr), pad], axis=1).astype(BF16)
    w_rest = w_in[:, o_b:].astype(BF16)
    wq = w_uq.reshape(A_QLORA, A_HEADS, A_NOPE + A_ROPE)
    wq_rope = wq[..., A_NOPE:]
    wq_p = jnp.concatenate([wq, _half_swap(wq_rope)], axis=-1).reshape(A_QLORA, A_HEADS * A_QBLK)
    wkv = w_ukv.reshape(A_KVLORA, A_HEADS, A_NOPE + A_VDIM)
    wkv_p = jnp.concatenate([wkv[..., :A_NOPE].reshape(A_KVLORA, A_WIDTH),
                             wkv[..., A_NOPE:].reshape(A_KVLORA, A_WIDTH)], axis=1)
    return (w_lat, w_rest), wq_p.astype(BF16), wkv_p.astype(BF16)


def _rope_table(seq):
    half = A_ROPE // 2
    pos = jnp.arange(seq, dtype=F32)
    freqs = ROPE_THETA ** (-jnp.arange(half, dtype=F32) / half)
    ang = pos[:, None] * freqs[None, :]
    cos, sin = jnp.cos(ang), jnp.sin(ang)
    return jnp.concatenate([cos, cos, sin, sin], axis=1)


def kernel(x, c, w_mod, b_mod, g_pre, g_post, ab_w_in, a_g_q, a_w_uq, a_g_kv, a_w_ukv,
           b_rel_bias, ab_w_out, sg_w_in, sg_ln_g, sg_ln_b, sg_w_s, sg_b_s, sg_w_out):
    bsz, seq, d = x.shape
    t = bsz * seq
    mod = _modulation(c, w_mod, b_mod)
    cs_tab = _rope_table(seq)
    x2 = x.reshape(t, d)

    def adaln_args(l):
        return g_pre[l][None], mod[l, :, None, d:2 * d], mod[l, :, None, :d]

    h = _prenorm(x2, *adaln_args(0), seq)
    for l in range(DEPTH):
        mgate = mod[l, :, None, 2 * d:]
        nxt = adaln_args(l + 1) if l + 1 < DEPTH else None
        i = l // 2
        if l % 2 == 0:
            w_in, w_uq, w_ukv = _prep_even_weights(ab_w_in[i], a_w_uq[i], a_w_ukv[i])
            z = _inproj(h, w_in)
            q, kn, v, kr = _latent(z, a_g_q[i][None], w_uq, a_g_kv[i][None], w_ukv, cs_tab, seq)
            oa = _mla_attention(q.reshape(bsz, seq, -1), kn.reshape(bsz, seq, -1),
                                kr.reshape(bsz, seq, -1), v.reshape(bsz, seq, -1))
            ob = _band_attention(z.reshape(bsz, seq, Z_W), _band_bias(b_rel_bias[i]))
            out = _outproj_even(oa.reshape(t, A_WIDTH), ob.reshape(t, B_WIDTH), z, x2,
                                ab_w_out[i], g_post[l][None], mgate, nxt, seq)
        else:
            z = _inproj(h, (sg_w_in[i],))
            out = _sgu_outproj(z, x2, sg_ln_g[i][None], sg_ln_b[i][None], sg_w_s[i],
                               sg_b_s[i].T, sg_w_out[i], g_post[l][None], mgate, nxt, seq)
        x2, h = out if nxt is not None else (out[0], None)
    return x2.reshape(bsz, seq, d)
```

```python
import functools

import jax
import jax.numpy as jnp
from jax import lax
from jax.experimental import pallas as pl
from jax.experimental.pallas import tpu as pltpu

D_MODEL = 2048
DEPTH = 4
CHUNK = 64
EPS = 1e-6
A_HEADS = 8
A_NOPE = 128
A_ROPE = 64
A_VDIM = 128
A_QLORA = 512
A_KVLORA = 256
ROPE_THETA = 10000.0
B_HEADS = 8
B_HDIM = 128
B_PREV = 8
REL_CLIP = 128
SG_GROUPS = 8
SG_LEN = 128
B_WIDTH = B_HEADS * B_HDIM
A_WIDTH = A_HEADS * A_VDIM

VMEM_LIMIT_BYTES = 56 * 1024 * 1024

LANES = 128

LATENT_W = 1024
Z_W = 6144
A_QBLK = 256
ATT_TQ = 256
BAND_KEYS = ATT_TQ + B_PREV * CHUNK

BF16 = jnp.bfloat16
F32 = jnp.float32


def _params(*sem):
    return pltpu.CompilerParams(dimension_semantics=sem, vmem_limit_bytes=VMEM_LIMIT_BYTES)


def _dot(a, b):
    return jnp.dot(a, b, preferred_element_type=F32)


def _dot_nt(a, b):
    return lax.dot_general(a, b, (((1,), (1,)), ((), ())), preferred_element_type=F32)


def _rms(x):
    return x * lax.rsqrt(jnp.mean(x * x, axis=-1, keepdims=True) + EPS)


def _silu(x):
    return x * jax.nn.sigmoid(x)


def _mod_kernel(c_ref, w_ref, b_ref, o_ref):
    cs = _silu(c_ref[...]).astype(BF16)
    o_ref[...] = _dot(cs, w_ref[...].astype(BF16)) + b_ref[...]


def _modulation(c, w_mod, b_mod, tn=512):
    depth, d, n = w_mod.shape
    bsz = c.shape[0]
    return pl.pallas_call(
        _mod_kernel,
        grid=(depth, n // tn),
        in_specs=[
            pl.BlockSpec((bsz, d), lambda l, j: (0, 0)),
            pl.BlockSpec((None, d, tn), lambda l, j: (l, 0, j)),
            pl.BlockSpec((None, 1, tn), lambda l, j: (l, 0, j)),
        ],
        out_specs=pl.BlockSpec((None, bsz, tn), lambda l, j: (l, 0, j)),
        out_shape=jax.ShapeDtypeStruct((depth, bsz, n), F32),
        compiler_params=_params("arbitrary", "arbitrary"),
        name="modulation",
    )(c, w_mod, b_mod.reshape(depth, 1, n))


def _adaln(x, g_ref, scale_ref, shift_ref):
    return (_rms(x) * g_ref[...] * (1.0 + scale_ref[...]) + shift_ref[...]).astype(BF16)


def _prenorm_kernel(x_ref, g_ref, scale_ref, shift_ref, h_ref):
    h_ref[...] = _adaln(x_ref[...], g_ref, scale_ref, shift_ref)


def _prenorm(x2, g_pre, scale, shift, seq, tm=512):
    t, d = x2.shape
    per_b = seq // tm
    vec = pl.BlockSpec((None, 1, d), lambda m: (m // per_b, 0, 0))
    return pl.pallas_call(
        _prenorm_kernel,
        grid=(t // tm,),
        in_specs=[pl.BlockSpec((tm, d), lambda m: (m, 0)),
                  pl.BlockSpec((1, d), lambda m: (0, 0)), vec, vec],
        out_specs=pl.BlockSpec((tm, d), lambda m: (m, 0)),
        out_shape=jax.ShapeDtypeStruct((t, d), BF16),
        compiler_params=_params("arbitrary"),
        name="prenorm",
    )(x2, g_pre, scale, shift)


def _inproj_kernel(carry, h_ref, *refs):
    if carry:
        wa_ref, w_ref, o_ref, wb_ref = refs
    else:
        w_ref, o_ref, wb_ref = refs
    n, m = pl.program_id(0), pl.program_id(1)
    tn = wb_ref.shape[1]

    @pl.when(m == 0)
    def _():
        if carry:
            @pl.when(n == 0)
            def _():
                wb_ref[...] = w_ref[...].astype(BF16)

            @pl.when(n > 0)
            def _():
                prev = wa_ref[:, wa_ref.shape[1] - carry:]
                w = jnp.concatenate([prev, w_ref[:, :tn - carry]], axis=1)
                wb_ref[...] = w.astype(BF16)
        else:
            wb_ref[...] = w_ref[...].astype(BF16)

    o_ref[...] = _dot(h_ref[...], wb_ref[...]).astype(BF16)


def _inproj(h, w_stack, layer, n_out, carry=0, tm=1024, tn=1024):
    t, d = h.shape
    in_specs = [pl.BlockSpec((tm, d), lambda n, m: (m, 0))]
    args = [h]
    if carry:
        pw = LANES * pl.cdiv(carry, LANES)
        in_specs.append(pl.BlockSpec((None, d, pw),
                                     lambda n, m: (layer, 0, jnp.maximum(n * (tn // pw) - 1, 0))))
        args.append(w_stack)
    in_specs.append(pl.BlockSpec((None, d, tn), lambda n, m: (layer, 0, n)))
    args.append(w_stack)
    return pl.pallas_call(
        functools.partial(_inproj_kernel, carry),
        grid=(n_out // tn, t // tm),
        in_specs=in_specs,
        out_specs=pl.BlockSpec((tm, tn), lambda n, m: (m, n)),
        out_shape=jax.ShapeDtypeStruct((t, n_out), BF16),
        scratch_shapes=[pltpu.VMEM((d, tn), BF16)],
        compiler_params=_params("arbitrary", "arbitrary"),
        name="inproj",
    )(*args)


def _latent_kernel(z_ref, gq_ref, wq_ref, gkv_ref, wkv_ref, cs_ref,
                   q_ref, kn_ref, v_ref, kr_ref):
    z = z_ref[...].astype(F32)
    cs = cs_ref[...]
    cq = (_rms(z[:, :A_QLORA]) * gq_ref[...]).astype(BF16)
    q = _dot(cq, wq_ref[...])
    qscale = float(A_NOPE + A_ROPE) ** -0.5
    for h in range(A_HEADS):
        lo = h * A_QBLK
        q_ref[:, lo:lo + A_NOPE] = (q[:, lo:lo + A_NOPE] * qscale).astype(BF16)
        pr = q[:, lo + A_NOPE:lo + A_QBLK] * cs
        pr = pr + pltpu.roll(pr, A_ROPE, 1)
        q_ref[:, lo + A_NOPE:lo + A_QBLK] = (pr * qscale).astype(BF16)
    ckv = (_rms(z[:, A_QLORA:A_QLORA + A_KVLORA]) * gkv_ref[...]).astype(BF16)
    kv = _dot(ckv, wkv_ref[...])
    kn_ref[...] = kv[:, :A_WIDTH].astype(BF16)
    v_ref[...] = kv[:, A_WIDTH:].astype(BF16)
    o = A_QLORA + A_KVLORA
    kr = z[:, o:o + 2 * A_ROPE]
    lane = lax.broadcasted_iota(jnp.int32, kr.shape, 1)
    half = A_ROPE // 2
    swapped = jnp.where(lane < half, -pltpu.roll(kr, 2 * A_ROPE - half, 1), pltpu.roll(kr, half, 1))
    roped = kr * cs + swapped * pltpu.roll(cs, A_ROPE, 1)
    kr_ref[...] = jnp.where(lane < A_ROPE, roped, 0.0).astype(BF16)


def _latent(z, g_q, w_uq, g_kv, w_ukv, cs_tab, seq, tm=512):
    t = z.shape[0]
    per_b = seq // tm
    const = lambda m: (0, 0)
    return pl.pallas_call(
        _latent_kernel,
        grid=(t // tm,),
        in_specs=[
            pl.BlockSpec((tm, LATENT_W), lambda m: (m, 0)),
            pl.BlockSpec((1, A_QLORA), const),
            pl.BlockSpec(w_uq.shape, const),
            pl.BlockSpec((1, A_KVLORA), const),
            pl.BlockSpec(w_ukv.shape, const),
            pl.BlockSpec((tm, 2 * A_ROPE), lambda m: (m % per_b, 0)),
        ],
        out_specs=[
            pl.BlockSpec((tm, A_HEADS * A_QBLK), lambda m: (m, 0)),
            pl.BlockSpec((tm, A_WIDTH), lambda m: (m, 0)),
            pl.BlockSpec((tm, A_WIDTH), lambda m: (m, 0)),
            pl.BlockSpec((tm, 2 * A_ROPE), lambda m: (m, 0)),
        ],
        out_shape=[
            jax.ShapeDtypeStruct((t, A_HEADS * A_QBLK), BF16),
            jax.ShapeDtypeStruct((t, A_WIDTH), BF16),
            jax.ShapeDtypeStruct((t, A_WIDTH), BF16),
            jax.ShapeDtypeStruct((t, 2 * A_ROPE), BF16),
        ],
        compiler_params=_params("arbitrary"),
        name="latent",
    )(z, g_q, w_uq, g_kv, w_ukv, cs_tab)


def _softmax_pv(s_parts, v_parts):
    m = functools.reduce(jnp.maximum, [jnp.max(s, axis=-1, keepdims=True) for s in s_parts])
    ps = [jnp.exp(s - m) for s in s_parts]
    den = functools.reduce(jnp.add, [jnp.sum(p, axis=-1, keepdims=True) for p in ps])
    acc = functools.reduce(jnp.add, [_dot(p.astype(BF16), v) for p, v in zip(ps, v_parts)])
    return acc / den


def _chunk_causal_mask(n):
    r = lax.broadcasted_iota(jnp.int32, (n, n), 0) // CHUNK
    c = lax.broadcasted_iota(jnp.int32, (n, n), 1) // CHUNK
    return c <= r


def _mla_attn_kernel(q_ref, kn_ref, kr_ref, v_ref, o_ref):
    seq = q_ref.shape[0]
    k = jnp.concatenate([kn_ref[...], kr_ref[...]], axis=1)
    mask = _chunk_causal_mask(ATT_TQ)
    for i in range(seq // ATT_TQ):
        lo = i * ATT_TQ
        q = q_ref[lo:lo + ATT_TQ, :]
        s_diag = jnp.where(mask, _dot_nt(q, k[lo:lo + ATT_TQ]), -jnp.inf)
        s_parts, v_parts = [s_diag], [v_ref[lo:lo + ATT_TQ, :]]
        if i > 0:
            s_parts.append(_dot_nt(q, k[:lo]))
            v_parts.append(v_ref[:lo, :])
        o_ref[lo:lo + ATT_TQ, :] = _softmax_pv(s_parts, v_parts).astype(BF16)


def _mla_attention(q, kn, kr, v):
    bsz, seq, _ = q.shape
    head = lambda b, h: (b, 0, h)
    return pl.pallas_call(
        _mla_attn_kernel,
        grid=(bsz, A_HEADS),
        in_specs=[
            pl.BlockSpec((None, seq, A_QBLK), head),
            pl.BlockSpec((None, seq, A_NOPE), head),
            pl.BlockSpec((None, seq, 2 * A_ROPE), lambda b, h: (b, 0, 0)),
            pl.BlockSpec((None, seq, A_VDIM), head),
        ],
        out_specs=pl.BlockSpec((None, seq, A_VDIM), head),
        out_shape=jax.ShapeDtypeStruct((bsz, seq, A_WIDTH), BF16),
        compiler_params=_params("arbitrary", "arbitrary"),
        name="mla_attention",
    )(q, kn, kr, v)


def _band_attn_kernel(q_ref, k_ref, v_ref, e_ref, o_ref, bias_ref):
    seq = q_ref.shape[0]
    scale = float(B_HDIM) ** -0.5

    @pl.when(pl.program_id(1) == 0)
    def _():
        e = jnp.broadcast_to(e_ref[...], (ATT_TQ, e_ref.shape[1]))
        toep = pltpu.roll(e, 0, 1, stride=1, stride_axis=0)[:, :BAND_KEYS]
        qc = lax.broadcasted_iota(jnp.int32, toep.shape, 0) // CHUNK
        kc = lax.broadcasted_iota(jnp.int32, toep.shape, 1) // CHUNK - B_PREV
        bias_ref[...] = jnp.where((kc <= qc) & (kc >= qc - B_PREV), toep, -jnp.inf)

    for i in range(seq // ATT_TQ):
        lo = i * ATT_TQ
        klo = max(0, lo - B_PREV * CHUNK)
        nk = lo + ATT_TQ - klo
        q = (q_ref[lo:lo + ATT_TQ, :].astype(F32) * scale).astype(BF16)
        s = _dot_nt(q, k_ref[klo:klo + nk, :]) + bias_ref[:, BAND_KEYS - nk:]
        o_ref[lo:lo + ATT_TQ, :] = _softmax_pv([s], [v_ref[klo:klo + nk, :]]).astype(BF16)


def _band_attention(z3, e_tab):
    bsz, seq, _ = z3.shape
    nb = LATENT_W // B_HDIM

    def col(off):
        return pl.BlockSpec((None, seq, B_HDIM), lambda h, b: (b, 0, off + h))

    return pl.pallas_call(
        _band_attn_kernel,
        grid=(B_HEADS, bsz),
        in_specs=[
            col(nb), col(nb + B_HEADS), col(nb + 2 * B_HEADS),
            pl.BlockSpec((None, 1, e_tab.shape[2]), lambda h, b: (h, 0, 0)),
        ],
        out_specs=pl.BlockSpec((None, seq, B_HDIM), lambda h, b: (b, 0, h)),
        out_shape=jax.ShapeDtypeStruct((bsz, seq, B_WIDTH), BF16),
        scratch_shapes=[pltpu.VMEM((ATT_TQ, BAND_KEYS), F32)],
        compiler_params=_params("arbitrary", "arbitrary"),
        name="band_attention",
    )(z3, z3, z3, e_tab)


def _band_bias_row(rel_table):
    h = rel_table.shape[0]
    width = 2 * (B_PREV * CHUNK)
    n_hi = B_PREV * CHUNK - REL_CLIP
    n_lo = width - n_hi - (2 * REL_CLIP + 1) - (CHUNK - 1)
    hi = rel_table[:, 2 * REL_CLIP:]
    lo = rel_table[:, :1]
    e = jnp.concatenate([jnp.broadcast_to(hi, (h, n_hi)), rel_table[:, ::-1],
                         jnp.broadcast_to(lo, (h, n_lo)),
                         jnp.broadcast_to(hi, (h, CHUNK - 1))], axis=1)
    return e.astype(F32)[:, None, :]


def _finish(y, refs, emit_h):
    x_ref, w_ref, gpost_ref, mgate_ref = refs[:4]
    o_ref, wb_ref = refs[7 if emit_h else 4], refs[-1]

    @pl.when(pl.program_id(0) == 0)
    def _():
        wb_ref[...] = w_ref[...].astype(BF16)

    r = _dot(y.astype(BF16), wb_ref[...])
    xn = x_ref[...] + mgate_ref[...] * (_rms(r) * gpost_ref[...])
    o_ref[...] = xn
    if emit_h:
        refs[8][...] = _adaln(xn, *refs[4:7])


def _finish_specs(x2, w_stack, layer, g_post, mgate, nxt, seq, tm):
    w = w_stack
    t, d = x2.shape
    per_b = seq // tm
    row = pl.BlockSpec((tm, d), lambda m: (m, 0))
    vec = pl.BlockSpec((1, d), lambda m: (0, 0))
    bvec = pl.BlockSpec((None, 1, d), lambda m: (m // per_b, 0, 0))
    whole = pl.BlockSpec((None, d, d), lambda m: (layer, 0, 0), pipeline_mode=pl.Buffered(1))
    args, in_specs = [x2, w, g_post, mgate], [row, whole, vec, bvec]
    out_specs, out_shape = [row], [jax.ShapeDtypeStruct((t, d), F32)]
    if nxt is not None:
        args += list(nxt)
        in_specs += [vec, bvec, bvec]
        out_specs.append(row)
        out_shape.append(jax.ShapeDtypeStruct((t, d), BF16))
    return args, in_specs, out_specs, out_shape, [pltpu.VMEM((d, d), BF16)]


def _outproj_even_kernel(emit_h, oa_ref, ob_ref, gate_ref, *refs):
    o = jnp.concatenate([oa_ref[...], ob_ref[...]], axis=1).astype(F32)
    y = o * _silu(gate_ref[...].astype(F32))
    _finish(y, refs, emit_h)


def _outproj_even(oa, ob, z, x2, w_stack, layer, g_post, mgate, nxt, seq, tm=256):
    t, d = x2.shape
    row = lambda m: (m, 0)
    args, in_specs, out_specs, out_shape, scratch = _finish_specs(
        x2, w_stack, layer, g_post, mgate, nxt, seq, tm)
    return pl.pallas_call(
        functools.partial(_outproj_even_kernel, nxt is not None),
        grid=(t // tm,),
        in_specs=[
            pl.BlockSpec((tm, A_WIDTH), row),
            pl.BlockSpec((tm, B_WIDTH), row),
            pl.BlockSpec((tm, d), lambda m: (m, 2)),
        ] + in_specs,
        out_specs=out_specs,
        out_shape=out_shape,
        scratch_shapes=scratch,
        compiler_params=_params("arbitrary"),
        name="outproj_even",
    )(oa, ob, z, *args)


def _sgu_kernel(emit_h, u_ref, v_ref, gate_ref, lng_ref, lnb_ref, ws_ref, bs_ref, *refs):
    sv_ref = refs[-2]
    tm, d = sv_ref.shape
    dg = d // SG_GROUPS
    v = v_ref[...].astype(F32)
    mu = jnp.mean(v, axis=-1, keepdims=True)
    vc = v - mu
    var = jnp.mean(vc * vc, axis=-1, keepdims=True)
    vn = ((vc * lax.rsqrt(var + EPS)) * lng_ref[...] + lnb_ref[...]).astype(BF16)
    cpos_r = lax.broadcasted_iota(jnp.int32, (SG_LEN, SG_LEN), 0) // CHUNK
    cpos_c = lax.broadcasted_iota(jnp.int32, (SG_LEN, SG_LEN), 1) // CHUNK
    mask = cpos_c <= cpos_r
    for g in range(SG_GROUPS):
        ws = jnp.where(mask, ws_ref[g], 0.0).astype(BF16)
        bs = bs_ref[:, g:g + 1]
        for n in range(tm // SG_LEN):
            blk = vn[n * SG_LEN:(n + 1) * SG_LEN, g * dg:(g + 1) * dg]
            sv_ref[n * SG_LEN:(n + 1) * SG_LEN, g * dg:(g + 1) * dg] = _dot(ws, blk) + bs
    y = u_ref[...].astype(F32) * sv_ref[...] * _silu(gate_ref[...].astype(F32))
    _finish(y, refs, emit_h)


def _sgu_outproj(z, x2, ln_g, ln_b, w_s, b_s_t, w_stack, layer, g_post, mgate, nxt, seq, tm=256):
    t, d = x2.shape
    const2 = lambda m: (0, 0)
    args, in_specs, out_specs, out_shape, scratch = _finish_specs(
        x2, w_stack, layer, g_post, mgate, nxt, seq, tm)
    return pl.pallas_call(
        functools.partial(_sgu_kernel, nxt is not None),
        grid=(t // tm,),
        in_specs=[
            pl.BlockSpec((tm, d), lambda m: (m, 0)),
            pl.BlockSpec((tm, d), lambda m: (m, 1)),
            pl.BlockSpec((tm, d), lambda m: (m, 2)),
            pl.BlockSpec((1, d), const2),
            pl.BlockSpec((1, d), const2),
            pl.BlockSpec(w_s.shape, lambda m: (0, 0, 0)),
            pl.BlockSpec(b_s_t.shape, const2),
        ] + in_specs,
        out_specs=out_specs,
        out_shape=out_shape,
        scratch_shapes=[pltpu.VMEM((tm, d), F32)] + scratch,
        compiler_params=_params("arbitrary"),
        name="sgu_outproj",
    )(z, z, z, ln_g, ln_b, w_s, b_s_t, *args)


def _half_swap(w):
    half = w.shape[-1] // 2
    return jnp.concatenate([-w[..., half:], w[..., :half]], axis=-1)


def _prep_latent_weights(w_uq, w_ukv):
    wq = w_uq.reshape(A_QLORA, A_HEADS, A_NOPE + A_ROPE)
    wq_rope = wq[..., A_NOPE:]
    wq_p = jnp.concatenate([wq, _half_swap(wq_rope)], axis=-1).reshape(A_QLORA, A_HEADS * A_QBLK)
    wkv = w_ukv.reshape(A_KVLORA, A_HEADS, A_NOPE + A_VDIM)
    wkv_p = jnp.concatenate([wkv[..., :A_NOPE].reshape(A_KVLORA, A_WIDTH),
                             wkv[..., A_NOPE:].reshape(A_KVLORA, A_WIDTH)], axis=1)
    return wq_p.astype(BF16), wkv_p.astype(BF16)


def _rope_table(seq):
    half = A_ROPE // 2
    pos = jnp.arange(seq, dtype=F32)
    freqs = ROPE_THETA ** (-jnp.arange(half, dtype=F32) / half)
    ang = pos[:, None] * freqs[None, :]
    cos, sin = jnp.cos(ang), jnp.sin(ang)
    return jnp.concatenate([cos, cos, sin, sin], axis=1)


def kernel(x, c, w_mod, b_mod, g_pre, g_post, ab_w_in, a_g_q, a_w_uq, a_g_kv, a_w_ukv,
           b_rel_bias, ab_w_out, sg_w_in, sg_ln_g, sg_ln_b, sg_w_s, sg_b_s, sg_w_out):
    bsz, seq, d = x.shape
    t = bsz * seq
    mod = _modulation(c, w_mod, b_mod)
    cs_tab = _rope_table(seq)
    x2 = x.reshape(t, d)

    def adaln_args(l):
        return g_pre[l][None], mod[l, :, None, d:2 * d], mod[l, :, None, :d]

    h = _prenorm(x2, *adaln_args(0), seq)
    for l in range(DEPTH):
        mgate = mod[l, :, None, 2 * d:]
        nxt = adaln_args(l + 1) if l + 1 < DEPTH else None
        i = l // 2
        if l % 2 == 0:
            w_uq, w_ukv = _prep_latent_weights(a_w_uq[i], a_w_ukv[i])
            z = _inproj(h, ab_w_in, i, Z_W, carry=LATENT_W - (A_QLORA + A_KVLORA + A_ROPE))
            q, kn, v, kr = _latent(z, a_g_q[i][None], w_uq, a_g_kv[i][None], w_ukv, cs_tab, seq)
            oa = _mla_attention(q.reshape(bsz, seq, -1), kn.reshape(bsz, seq, -1),
                                kr.reshape(bsz, seq, -1), v.reshape(bsz, seq, -1))
            ob = _band_attention(z.reshape(bsz, seq, Z_W), _band_bias_row(b_rel_bias[i]))
            out = _outproj_even(oa.reshape(t, A_WIDTH), ob.reshape(t, B_WIDTH), z, x2,
                                ab_w_out, i, g_post[l][None], mgate, nxt, seq)
        else:
            z = _inproj(h, sg_w_in, i, sg_w_in.shape[2])
            out = _sgu_outproj(z, x2, sg_ln_g[i][None], sg_ln_b[i][None], sg_w_s[i],
                               sg_b_s[i].T, sg_w_out, i, g_post[l][None], mgate, nxt, seq)
        x2, h = out if nxt is not None else (out[0], None)
    return x2.reshape(bsz, seq, d)
```

```python
import functools

import jax
import jax.numpy as jnp
from jax import lax
from jax.experimental import pallas as pl
from jax.experimental.pallas import tpu as pltpu

D_MODEL = 2048
DEPTH = 4
CHUNK = 64
EPS = 1e-6
A_HEADS = 8
A_NOPE = 128
A_ROPE = 64
A_VDIM = 128
A_QLORA = 512
A_KVLORA = 256
ROPE_THETA = 10000.0
B_HEADS = 8
B_HDIM = 128
B_PREV = 8
REL_CLIP = 128
SG_GROUPS = 8
SG_LEN = 128
B_WIDTH = B_HEADS * B_HDIM
A_WIDTH = A_HEADS * A_VDIM

VMEM_LIMIT_BYTES = 56 * 1024 * 1024

SUBLANES = 8

LATENT_W = 1024
Z_W = 6144
A_QBLK = 256
ATT_TQ = 256
BAND_KEYS = ATT_TQ + B_PREV * CHUNK

BF16 = jnp.bfloat16
F32 = jnp.float32


def _params(*sem):
    return pltpu.CompilerParams(dimension_semantics=sem, vmem_limit_bytes=VMEM_LIMIT_BYTES)


def _dot(a, b):
    return jnp.dot(a, b, preferred_element_type=F32)


def _dot_nt(a, b):
    return lax.dot_general(a, b, (((1,), (1,)), ((), ())), preferred_element_type=F32)


def _rms(x):
    return x * lax.rsqrt(jnp.mean(x * x, axis=-1, keepdims=True) + EPS)


def _silu(x):
    return x * jax.nn.sigmoid(x)


def _mod_kernel(c_ref, w_ref, b_ref, o_ref):
    cs = _silu(c_ref[...]).astype(BF16)
    o_ref[...] = _dot(cs, w_ref[...].astype(BF16)) + b_ref[...]


def _modulation(c, w_mod, b_mod, tn=512):
    depth, d, n = w_mod.shape
    bsz = c.shape[0]
    return pl.pallas_call(
        _mod_kernel,
        grid=(depth, n // tn),
        in_specs=[
            pl.BlockSpec((bsz, d), lambda l, j: (0, 0)),
            pl.BlockSpec((None, d, tn), lambda l, j: (l, 0, j)),
            pl.BlockSpec((None, 1, tn), lambda l, j: (l, 0, j)),
        ],
        out_specs=pl.BlockSpec((None, bsz, tn), lambda l, j: (l, 0, j)),
        out_shape=jax.ShapeDtypeStruct((depth, bsz, n), F32),
        compiler_params=_params("arbitrary", "arbitrary"),
        name="modulation",
    )(c, w_mod, b_mod.reshape(depth, 1, n))


def _adaln(x, g_ref, scale_ref, shift_ref):
    return (_rms(x) * g_ref[...] * (1.0 + scale_ref[...]) + shift_ref[...]).astype(BF16)


def _prenorm_kernel(x_ref, g_ref, scale_ref, shift_ref, h_ref):
    h_ref[...] = _adaln(x_ref[...], g_ref, scale_ref, shift_ref)


def _prenorm(x2, g_pre, scale, shift, seq, tm=512):
    t, d = x2.shape
    per_b = seq // tm
    vec = pl.BlockSpec((None, 1, d), lambda m: (m // per_b, 0, 0))
    return pl.pallas_call(
        _prenorm_kernel,
        grid=(t // tm,),
        in_specs=[pl.BlockSpec((tm, d), lambda m: (m, 0)),
                  pl.BlockSpec((1, d), lambda m: (0, 0)), vec, vec],
        out_specs=pl.BlockSpec((tm, d), lambda m: (m, 0)),
        out_shape=jax.ShapeDtypeStruct((t, d), BF16),
        compiler_params=_params("arbitrary"),
        name="prenorm",
    )(x2, g_pre, scale, shift)


def _inproj_kernel(transposed, h_ref, w_ref, o_ref, wb_ref):
    @pl.when(pl.program_id(1) == 0)
    def _():
        wb_ref[...] = (w_ref[0].T if transposed else w_ref[...]).astype(BF16)

    o_ref[...] = _dot(h_ref[...], wb_ref[...]).astype(BF16)


def _inproj(h, w_stack, layer, n_out, transposed=False, carry=0, tm=1024, tn=1024):
    t, d = h.shape
    if transposed:
        w_spec = pl.BlockSpec(
            (pl.Element(1), pl.Element(tn), pl.Element(d)),
            lambda n, m: (layer, pl.multiple_of(jnp.maximum(n * tn - carry, 0), SUBLANES), 0))
    else:
        assert carry == 0
        w_spec = pl.BlockSpec((None, d, tn), lambda n, m: (layer, 0, n))
    return pl.pallas_call(
        functools.partial(_inproj_kernel, transposed),
        grid=(n_out // tn, t // tm),
        in_specs=[pl.BlockSpec((tm, d), lambda n, m: (m, 0)), w_spec],
        out_specs=pl.BlockSpec((tm, tn), lambda n, m: (m, n)),
        out_shape=jax.ShapeDtypeStruct((t, n_out), BF16),
        scratch_shapes=[pltpu.VMEM((d, tn), BF16)],
        compiler_params=_params("arbitrary", "arbitrary"),
        name="inproj",
    )(h, w_stack)


def _latent_kernel(z_ref, gq_ref, wq_ref, gkv_ref, wkv_ref, cs_ref,
                   q_ref, kn_ref, v_ref, kr_ref):
    z = z_ref[...].astype(F32)
    cs = cs_ref[...]
    cq = (_rms(z[:, :A_QLORA]) * gq_ref[...]).astype(BF16)
    q = _dot(cq, wq_ref[...])
    qscale = float(A_NOPE + A_ROPE) ** -0.5
    for h in range(A_HEADS):
        lo = h * A_QBLK
        q_ref[:, lo:lo + A_NOPE] = (q[:, lo:lo + A_NOPE] * qscale).astype(BF16)
        pr = q[:, lo + A_NOPE:lo + A_QBLK] * cs
        pr = pr + pltpu.roll(pr, A_ROPE, 1)
        q_ref[:, lo + A_NOPE:lo + A_QBLK] = (pr * qscale).astype(BF16)
    ckv = (_rms(z[:, A_QLORA:A_QLORA + A_KVLORA]) * gkv_ref[...]).astype(BF16)
    kv = _dot(ckv, wkv_ref[...])
    kn_ref[...] = kv[:, :A_WIDTH].astype(BF16)
    v_ref[...] = kv[:, A_WIDTH:].astype(BF16)
    o = A_QLORA + A_KVLORA
    kr = z[:, o:o + 2 * A_ROPE]
    lane = lax.broadcasted_iota(jnp.int32, kr.shape, 1)
    half = A_ROPE // 2
    swapped = jnp.where(lane < half, -pltpu.roll(kr, 2 * A_ROPE - half, 1), pltpu.roll(kr, half, 1))
    roped = kr * cs + swapped * pltpu.roll(cs, A_ROPE, 1)
    kr_ref[...] = jnp.where(lane < A_ROPE, roped, 0.0).astype(BF16)


def _latent(z, g_q, w_uq, g_kv, w_ukv, cs_tab, seq, tm=512):
    t = z.shape[0]
    per_b = seq // tm
    const = lambda m: (0, 0)
    return pl.pallas_call(
        _latent_kernel,
        grid=(t // tm,),
        in_specs=[
            pl.BlockSpec((tm, LATENT_W), lambda m: (m, 0)),
            pl.BlockSpec((1, A_QLORA), const),
            pl.BlockSpec(w_uq.shape, const),
            pl.BlockSpec((1, A_KVLORA), const),
            pl.BlockSpec(w_ukv.shape, const),
            pl.BlockSpec((tm, 2 * A_ROPE), lambda m: (m % per_b, 0)),
        ],
        out_specs=[
            pl.BlockSpec((tm, A_HEADS * A_QBLK), lambda m: (m, 0)),
            pl.BlockSpec((tm, A_WIDTH), lambda m: (m, 0)),
            pl.BlockSpec((tm, A_WIDTH), lambda m: (m, 0)),
            pl.BlockSpec((tm, 2 * A_ROPE), lambda m: (m, 0)),
        ],
        out_shape=[
            jax.ShapeDtypeStruct((t, A_HEADS * A_QBLK), BF16),
            jax.ShapeDtypeStruct((t, A_WIDTH), BF16),
            jax.ShapeDtypeStruct((t, A_WIDTH), BF16),
            jax.ShapeDtypeStruct((t, 2 * A_ROPE), BF16),
        ],
        compiler_params=_params("arbitrary"),
        name="latent",
    )(z, g_q, w_uq, g_kv, w_ukv, cs_tab)


def _softmax_pv(s_parts, v_parts):
    m = functools.reduce(jnp.maximum, [jnp.max(s, axis=-1, keepdims=True) for s in s_parts])
    ps = [jnp.exp(s - m) for s in s_parts]
    den = functools.reduce(jnp.add, [jnp.sum(p, axis=-1, keepdims=True) for p in ps])
    acc = functools.reduce(jnp.add, [_dot(p.astype(BF16), v) for p, v in zip(ps, v_parts)])
    return acc / den


def _chunk_causal_mask(n):
    r = lax.broadcasted_iota(jnp.int32, (n, n), 0) // CHUNK
    c = lax.broadcasted_iota(jnp.int32, (n, n), 1) // CHUNK
    return c <= r


def _mla_attn_kernel(q_ref, kn_ref, kr_ref, v_ref, o_ref):
    seq = q_ref.shape[0]
    k = jnp.concatenate([kn_ref[...], kr_ref[...]], axis=1)
    mask = _chunk_causal_mask(ATT_TQ)
    for i in range(seq // ATT_TQ):
        lo = i * ATT_TQ
        q = q_ref[lo:lo + ATT_TQ, :]
        s_diag = jnp.where(mask, _dot_nt(q, k[lo:lo + ATT_TQ]), -jnp.inf)
        s_parts, v_parts = [s_diag], [v_ref[lo:lo + ATT_TQ, :]]
        if i > 0:
            s_parts.append(_dot_nt(q, k[:lo]))
            v_parts.append(v_ref[:lo, :])
        o_ref[lo:lo + ATT_TQ, :] = _softmax_pv(s_parts, v_parts).astype(BF16)


def _mla_attention(q, kn, kr, v):
    bsz, seq, _ = q.shape
    head = lambda b, h: (b, 0, h)
    return pl.pallas_call(
        _mla_attn_kernel,
        grid=(bsz, A_HEADS),
        in_specs=[
            pl.BlockSpec((None, seq, A_QBLK), head),
            pl.BlockSpec((None, seq, A_NOPE), head),
            pl.BlockSpec((None, seq, 2 * A_ROPE), lambda b, h: (b, 0, 0)),
            pl.BlockSpec((None, seq, A_VDIM), head),
        ],
        out_specs=pl.BlockSpec((None, seq, A_VDIM), head),
        out_shape=jax.ShapeDtypeStruct((bsz, seq, A_WIDTH), BF16),
        compiler_params=_params("arbitrary", "arbitrary"),
        name="mla_attention",
    )(q, kn, kr, v)


def _band_attn_kernel(q_ref, k_ref, v_ref, e_ref, o_ref, bias_ref):
    seq = q_ref.shape[0]
    scale = float(B_HDIM) ** -0.5

    @pl.when(pl.program_id(1) == 0)
    def _():
        e = jnp.broadcast_to(e_ref[...], (ATT_TQ, e_ref.shape[1]))
        toep = pltpu.roll(e, 0, 1, stride=1, stride_axis=0)[:, :BAND_KEYS]
        qc = lax.broadcasted_iota(jnp.int32, toep.shape, 0) // CHUNK
        kc = lax.broadcasted_iota(jnp.int32, toep.shape, 1) // CHUNK - B_PREV
        bias_ref[...] = jnp.where((kc <= qc) & (kc >= qc - B_PREV), toep, -jnp.inf)

    for i in range(seq // ATT_TQ):
        lo = i * ATT_TQ
        klo = max(0, lo - B_PREV * CHUNK)
        nk = lo + ATT_TQ - klo
        q = (q_ref[lo:lo + ATT_TQ, :].astype(F32) * scale).astype(BF16)
        s = _dot_nt(q, k_ref[klo:klo + nk, :]) + bias_ref[:, BAND_KEYS - nk:]
        o_ref[lo:lo + ATT_TQ, :] = _softmax_pv([s], [v_ref[klo:klo + nk, :]]).astype(BF16)


def _band_attention(z3, e_tab):
    bsz, seq, _ = z3.shape
    nb = LATENT_W // B_HDIM

    def col(off):
        return pl.BlockSpec((None, seq, B_HDIM), lambda h, b: (b, 0, off + h))

    return pl.pallas_call(
        _band_attn_kernel,
        grid=(B_HEADS, bsz),
        in_specs=[
            col(nb), col(nb + B_HEADS), col(nb + 2 * B_HEADS),
            pl.BlockSpec((None, 1, e_tab.shape[2]), lambda h, b: (h, 0, 0)),
        ],
        out_specs=pl.BlockSpec((None, seq, B_HDIM), lambda h, b: (b, 0, h)),
        out_shape=jax.ShapeDtypeStruct((bsz, seq, B_WIDTH), BF16),
        scratch_shapes=[pltpu.VMEM((ATT_TQ, BAND_KEYS), F32)],
        compiler_params=_params("arbitrary", "arbitrary"),
        name="band_attention",
    )(z3, z3, z3, e_tab)


def _band_bias_row(rel_table):
    h = rel_table.shape[0]
    width = 2 * (B_PREV * CHUNK)
    n_hi = B_PREV * CHUNK - REL_CLIP
    n_lo = width - n_hi - (2 * REL_CLIP + 1) - (CHUNK - 1)
    hi = rel_table[:, 2 * REL_CLIP:]
    lo = rel_table[:, :1]
    e = jnp.concatenate([jnp.broadcast_to(hi, (h, n_hi)), rel_table[:, ::-1],
                         jnp.broadcast_to(lo, (h, n_lo)),
                         jnp.broadcast_to(hi, (h, CHUNK - 1))], axis=1)
    return e.astype(F32)[:, None, :]


def _finish(y, refs, emit_h):
    x_ref, w_ref, gpost_ref, mgate_ref = refs[:4]
    o_ref, wb_ref = refs[7 if emit_h else 4], refs[-1]

    @pl.when(pl.program_id(0) == 0)
    def _():
        wb_ref[...] = w_ref[...].astype(BF16)

    r = _dot(y.astype(BF16), wb_ref[...])
    xn = x_ref[...] + mgate_ref[...] * (_rms(r) * gpost_ref[...])
    o_ref[...] = xn
    if emit_h:
        refs[8][...] = _adaln(xn, *refs[4:7])


def _finish_specs(x2, w_stack, layer, g_post, mgate, nxt, seq, tm):
    w = w_stack
    t, d = x2.shape
    per_b = seq // tm
    row = pl.BlockSpec((tm, d), lambda m: (m, 0))
    vec = pl.BlockSpec((1, d), lambda m: (0, 0))
    bvec = pl.BlockSpec((None, 1, d), lambda m: (m // per_b, 0, 0))
    whole = pl.BlockSpec((None, d, d), lambda m: (layer, 0, 0), pipeline_mode=pl.Buffered(1))
    args, in_specs = [x2, w, g_post, mgate], [row, whole, vec, bvec]
    out_specs, out_shape = [row], [jax.ShapeDtypeStruct((t, d), F32)]
    if nxt is not None:
        args += list(nxt)
        in_specs += [vec, bvec, bvec]
        out_specs.append(row)
        out_shape.append(jax.ShapeDtypeStruct((t, d), BF16))
    return args, in_specs, out_specs, out_shape, [pltpu.VMEM((d, d), BF16)]


def _outproj_even_kernel(emit_h, oa_ref, ob_ref, gate_ref, *refs):
    o = jnp.concatenate([oa_ref[...], ob_ref[...]], axis=1).astype(F32)
    y = o * _silu(gate_ref[...].astype(F32))
    _finish(y, refs, emit_h)


def _outproj_even(oa, ob, z, x2, w_stack, layer, g_post, mgate, nxt, seq, tm=256):
    t, d = x2.shape
    row = lambda m: (m, 0)
    args, in_specs, out_specs, out_shape, scratch = _finish_specs(
        x2, w_stack, layer, g_post, mgate, nxt, seq, tm)
    return pl.pallas_call(
        functools.partial(_outproj_even_kernel, nxt is not None),
        grid=(t // tm,),
        in_specs=[
            pl.BlockSpec((tm, A_WIDTH), row),
            pl.BlockSpec((tm, B_WIDTH), row),
            pl.BlockSpec((tm, d), lambda m: (m, 2)),
        ] + in_specs,
        out_specs=out_specs,
        out_shape=out_shape,
        scratch_shapes=scratch,
        compiler_params=_params("arbitrary"),
        name="outproj_even",
    )(oa, ob, z, *args)


def _sgu_kernel(emit_h, u_ref, v_ref, gate_ref, lng_ref, lnb_ref, ws_ref, bs_ref, *refs):
    sv_ref = refs[-2]
    tm, d = sv_ref.shape
    dg = d // SG_GROUPS
    v = v_ref[...].astype(F32)
    mu = jnp.mean(v, axis=-1, keepdims=True)
    vc = v - mu
    var = jnp.mean(vc * vc, axis=-1, keepdims=True)
    vn = ((vc * lax.rsqrt(var + EPS)) * lng_ref[...] + lnb_ref[...]).astype(BF16)
    cpos_r = lax.broadcasted_iota(jnp.int32, (SG_LEN, SG_LEN), 0) // CHUNK
    cpos_c = lax.broadcasted_iota(jnp.int32, (SG_LEN, SG_LEN), 1) // CHUNK
    mask = cpos_c <= cpos_r
    for g in range(SG_GROUPS):
        ws = jnp.where(mask, ws_ref[g], 0.0).astype(BF16)
        bs = bs_ref[:, g:g + 1]
        for n in range(tm // SG_LEN):
            blk = vn[n * SG_LEN:(n + 1) * SG_LEN, g * dg:(g + 1) * dg]
            sv_ref[n * SG_LEN:(n + 1) * SG_LEN, g * dg:(g + 1) * dg] = _dot(ws, blk) + bs
    y = u_ref[...].astype(F32) * sv_ref[...] * _silu(gate_ref[...].astype(F32))
    _finish(y, refs, emit_h)


def _sgu_outproj(z, x2, ln_g, ln_b, w_s, b_s_t, w_stack, layer, g_post, mgate, nxt, seq, tm=256):
    t, d = x2.shape
    const2 = lambda m: (0, 0)
    args, in_specs, out_specs, out_shape, scratch = _finish_specs(
        x2, w_stack, layer, g_post, mgate, nxt, seq, tm)
    return pl.pallas_call(
        functools.partial(_sgu_kernel, nxt is not None),
        grid=(t // tm,),
        in_specs=[
            pl.BlockSpec((tm, d), lambda m: (m, 0)),
            pl.BlockSpec((tm, d), lambda m: (m, 1)),
            pl.BlockSpec((tm, d), lambda m: (m, 2)),
            pl.BlockSpec((1, d), const2),
            pl.BlockSpec((1, d), const2),
            pl.BlockSpec(w_s.shape, lambda m: (0, 0, 0)),
            pl.BlockSpec(b_s_t.shape, const2),
        ] + in_specs,
        out_specs=out_specs,
        out_shape=out_shape,
        scratch_shapes=[pltpu.VMEM((tm, d), F32)] + scratch,
        compiler_params=_params("arbitrary"),
        name="sgu_outproj",
    )(z, z, z, ln_g, ln_b, w_s, b_s_t, *args)


def _half_swap(w):
    half = w.shape[-1] // 2
    return jnp.concatenate([-w[..., half:], w[..., :half]], axis=-1)


def _prep_latent_weights(w_uq, w_ukv):
    wq = w_uq.reshape(A_QLORA, A_HEADS, A_NOPE + A_ROPE)
    wq_rope = wq[..., A_NOPE:]
    wq_p = jnp.concatenate([wq, _half_swap(wq_rope)], axis=-1).reshape(A_QLORA, A_HEADS * A_QBLK)
    wkv = w_ukv.reshape(A_KVLORA, A_HEADS, A_NOPE + A_VDIM)
    wkv_p = jnp.concatenate([wkv[..., :A_NOPE].reshape(A_KVLORA, A_WIDTH),
                             wkv[..., A_NOPE:].reshape(A_KVLORA, A_WIDTH)], axis=1)
    return wq_p.astype(BF16), wkv_p.astype(BF16)


def _rope_table(seq):
    half = A_ROPE // 2
    pos = jnp.arange(seq, dtype=F32)
    freqs = ROPE_THETA ** (-jnp.arange(half, dtype=F32) / half)
    ang = pos[:, None] * freqs[None, :]
    cos, sin = jnp.cos(ang), jnp.sin(ang)
    return jnp.concatenate([cos, cos, sin, sin], axis=1)


def kernel(x, c, w_mod, b_mod, g_pre, g_post, ab_w_in, a_g_q, a_w_uq, a_g_kv, a_w_ukv,
           b_rel_bias, ab_w_out, sg_w_in, sg_ln_g, sg_ln_b, sg_w_s, sg_b_s, sg_w_out):
    bsz, seq, d = x.shape
    t = bsz * seq
    mod = _modulation(c, w_mod, b_mod)
    cs_tab = _rope_table(seq)
    x2 = x.reshape(t, d)

    def adaln_args(l):
        return g_pre[l][None], mod[l, :, None, d:2 * d], mod[l, :, None, :d]

    ab_w_in_t = jnp.swapaxes(ab_w_in, 1, 2)
    h = _prenorm(x2, *adaln_args(0), seq)
    for l in range(DEPTH):
        mgate = mod[l, :, None, 2 * d:]
        nxt = adaln_args(l + 1) if l + 1 < DEPTH else None
        i = l // 2
        if l % 2 == 0:
            w_uq, w_ukv = _prep_latent_weights(a_w_uq[i], a_w_ukv[i])
            z = _inproj(h, ab_w_in_t, i, Z_W, transposed=True,
                        carry=LATENT_W - (A_QLORA + A_KVLORA + A_ROPE))
            q, kn, v, kr = _latent(z, a_g_q[i][None], w_uq, a_g_kv[i][None], w_ukv, cs_tab, seq)
            oa = _mla_attention(q.reshape(bsz, seq, -1), kn.reshape(bsz, seq, -1),
                                kr.reshape(bsz, seq, -1), v.reshape(bsz, seq, -1))
            ob = _band_attention(z.reshape(bsz, seq, Z_W), _band_bias_row(b_rel_bias[i]))
            out = _outproj_even(oa.reshape(t, A_WIDTH), ob.reshape(t, B_WIDTH), z, x2,
                                ab_w_out, i, g_post[l][None], mgate, nxt, seq)
        else:
            z = _inproj(h, sg_w_in, i, sg_w_in.shape[2])
            out = _sgu_outproj(z, x2, sg_ln_g[i][None], sg_ln_b[i][None], sg_w_s[i],
                               sg_b_s[i].T, sg_w_out, i, g_post[l][None], mgate, nxt, seq)
        x2, h = out if nxt is not None else (out[0], None)
    return x2.reshape(bsz, seq, d)
```

```python
import functools

import jax
import jax.numpy as jnp
from jax import lax
from jax.experimental import pallas as pl
from jax.experimental.pallas import tpu as pltpu

D_MODEL = 2048
DEPTH = 4
CHUNK = 64
EPS = 1e-6
A_HEADS = 8
A_NOPE = 128
A_ROPE = 64
A_VDIM = 128
A_QLORA = 512
A_KVLORA = 256
ROPE_THETA = 10000.0
B_HEADS = 8
B_HDIM = 128
B_PREV = 8
REL_CLIP = 128
SG_GROUPS = 8
SG_LEN = 128
B_WIDTH = B_HEADS * B_HDIM
A_WIDTH = A_HEADS * A_VDIM

VMEM_LIMIT_BYTES = 56 * 1024 * 1024

SUBLANES = 8

LATENT_W = 1024
Z_W = 6144
A_QBLK = 256
ATT_TQ = 256
BAND_KEYS = ATT_TQ + B_PREV * CHUNK

BF16 = jnp.bfloat16
F32 = jnp.float32


def _params(*sem):
    return pltpu.CompilerParams(dimension_semantics=sem, vmem_limit_bytes=VMEM_LIMIT_BYTES)


def _dot(a, b):
    return jnp.dot(a, b, preferred_element_type=F32)


def _dot_nt(a, b):
    return lax.dot_general(a, b, (((1,), (1,)), ((), ())), preferred_element_type=F32)


def _rms(x):
    return x * lax.rsqrt(jnp.mean(x * x, axis=-1, keepdims=True) + EPS)


def _silu(x):
    return x * jax.nn.sigmoid(x)


def _mod_kernel(c_ref, w_ref, b_ref, o_ref):
    cs = _silu(c_ref[...]).astype(BF16)
    o_ref[...] = _dot(cs, w_ref[...].astype(BF16)) + b_ref[...]


def _modulation(c, w_mod, b_mod, tn=512):
    depth, d, n = w_mod.shape
    bsz = c.shape[0]
    return pl.pallas_call(
        _mod_kernel,
        grid=(depth, n // tn),
        in_specs=[
            pl.BlockSpec((bsz, d), lambda l, j: (0, 0)),
            pl.BlockSpec((None, d, tn), lambda l, j: (l, 0, j)),
            pl.BlockSpec((None, 1, tn), lambda l, j: (l, 0, j)),
        ],
        out_specs=pl.BlockSpec((None, bsz, tn), lambda l, j: (l, 0, j)),
        out_shape=jax.ShapeDtypeStruct((depth, bsz, n), F32),
        compiler_params=_params("arbitrary", "arbitrary"),
        name="modulation",
    )(c, w_mod, b_mod.reshape(depth, 1, n))


def _adaln(x, g_ref, scale_ref, shift_ref):
    return (_rms(x) * g_ref[...] * (1.0 + scale_ref[...]) + shift_ref[...]).astype(BF16)


def _prenorm_kernel(x_ref, g_ref, scale_ref, shift_ref, h_ref):
    h_ref[...] = _adaln(x_ref[...], g_ref, scale_ref, shift_ref)


def _prenorm(x2, g_pre, scale, shift, seq, tm=512):
    t, d = x2.shape
    per_b = seq // tm
    vec = pl.BlockSpec((None, 1, d), lambda m: (m // per_b, 0, 0))
    return pl.pallas_call(
        _prenorm_kernel,
        grid=(t // tm,),
        in_specs=[pl.BlockSpec((tm, d), lambda m: (m, 0)),
                  pl.BlockSpec((1, d), lambda m: (0, 0)), vec, vec],
        out_specs=pl.BlockSpec((tm, d), lambda m: (m, 0)),
        out_shape=jax.ShapeDtypeStruct((t, d), BF16),
        compiler_params=_params("arbitrary"),
        name="prenorm",
    )(x2, g_pre, scale, shift)


def _inproj_kernel(transposed, h_ref, w_ref, o_ref, wb_ref):
    @pl.when(pl.program_id(1) == 0)
    def _():
        wb_ref[...] = (w_ref[0].T if transposed else w_ref[...]).astype(BF16)

    o_ref[...] = _dot(h_ref[...], wb_ref[...]).astype(BF16)


def _inproj(h, w_stack, layer, n_out, transposed=False, carry=0, tm=1024, tn=1024):
    t, d = h.shape
    if transposed:
        w_spec = pl.BlockSpec(
            (pl.Element(1), pl.Element(tn), pl.Element(d)),
            lambda n, m: (layer, pl.multiple_of(jnp.maximum(n * tn - carry, 0), SUBLANES), 0))
    else:
        assert carry == 0
        w_spec = pl.BlockSpec((None, d, tn), lambda n, m: (layer, 0, n))
    return pl.pallas_call(
        functools.partial(_inproj_kernel, transposed),
        grid=(n_out // tn, t // tm),
        in_specs=[pl.BlockSpec((tm, d), lambda n, m: (m, 0)), w_spec],
        out_specs=pl.BlockSpec((tm, tn), lambda n, m: (m, n)),
        out_shape=jax.ShapeDtypeStruct((t, n_out), BF16),
        scratch_shapes=[pltpu.VMEM((d, tn), BF16)],
        compiler_params=_params("arbitrary", "arbitrary"),
        name="inproj",
    )(h, w_stack)


def _latent_kernel(z_ref, gq_ref, wq_ref, gkv_ref, wkv_ref, cs_ref,
                   q_ref, kn_ref, v_ref, kr_ref):
    z = z_ref[...].astype(F32)
    cs = cs_ref[...]
    cq = (_rms(z[:, :A_QLORA]) * gq_ref[...]).astype(BF16)
    q = _dot(cq, wq_ref[...])
    qscale = float(A_NOPE + A_ROPE) ** -0.5
    for h in range(A_HEADS):
        lo = h * A_QBLK
        q_ref[:, lo:lo + A_NOPE] = (q[:, lo:lo + A_NOPE] * qscale).astype(BF16)
        pr = q[:, lo + A_NOPE:lo + A_QBLK] * cs
        pr = pr + pltpu.roll(pr, A_ROPE, 1)
        q_ref[:, lo + A_NOPE:lo + A_QBLK] = (pr * qscale).astype(BF16)
    ckv = (_rms(z[:, A_QLORA:A_QLORA + A_KVLORA]) * gkv_ref[...]).astype(BF16)
    kv = _dot(ckv, wkv_ref[...])
    kn_ref[...] = kv[:, :A_WIDTH].astype(BF16)
    v_ref[...] = kv[:, A_WIDTH:].astype(BF16)
    o = A_QLORA + A_KVLORA
    kr = z[:, o:o + 2 * A_ROPE]
    lane = lax.broadcasted_iota(jnp.int32, kr.shape, 1)
    half = A_ROPE // 2
    swapped = jnp.where(lane < half, -pltpu.roll(kr, 2 * A_ROPE - half, 1), pltpu.roll(kr, half, 1))
    roped = kr * cs + swapped * pltpu.roll(cs, A_ROPE, 1)
    kr_ref[...] = jnp.where(lane < A_ROPE, roped, 0.0).astype(BF16)


def _latent(z, g_q, w_uq, g_kv, w_ukv, cs_tab, seq, tm=512):
    t = z.shape[0]
    per_b = seq // tm
    const = lambda m: (0, 0)
    return pl.pallas_call(
        _latent_kernel,
        grid=(t // tm,),
        in_specs=[
            pl.BlockSpec((tm, LATENT_W), lambda m: (m, 0)),
            pl.BlockSpec((1, A_QLORA), const),
            pl.BlockSpec(w_uq.shape, const),
            pl.BlockSpec((1, A_KVLORA), const),
            pl.BlockSpec(w_ukv.shape, const),
            pl.BlockSpec((tm, 2 * A_ROPE), lambda m: (m % per_b, 0)),
        ],
        out_specs=[
            pl.BlockSpec((tm, A_HEADS * A_QBLK), lambda m: (m, 0)),
            pl.BlockSpec((tm, A_WIDTH), lambda m: (m, 0)),
            pl.BlockSpec((tm, A_WIDTH), lambda m: (m, 0)),
            pl.BlockSpec((tm, 2 * A_ROPE), lambda m: (m, 0)),
        ],
        out_shape=[
            jax.ShapeDtypeStruct((t, A_HEADS * A_QBLK), BF16),
            jax.ShapeDtypeStruct((t, A_WIDTH), BF16),
            jax.ShapeDtypeStruct((t, A_WIDTH), BF16),
            jax.ShapeDtypeStruct((t, 2 * A_ROPE), BF16),
        ],
        compiler_params=_params("arbitrary"),
        name="latent",
    )(z, g_q, w_uq, g_kv, w_ukv, cs_tab)


def _softmax_pv(s_parts, v_parts):
    m = functools.reduce(jnp.maximum, [jnp.max(s, axis=-1, keepdims=True) for s in s_parts])
    ps = [jnp.exp(s - m) for s in s_parts]
    den = functools.reduce(jnp.add, [jnp.sum(p, axis=-1, keepdims=True) for p in ps])
    acc = functools.reduce(jnp.add, [_dot(p.astype(BF16), v) for p, v in zip(ps, v_parts)])
    return acc / den


def _chunk_causal_mask(n):
    r = lax.broadcasted_iota(jnp.int32, (n, n), 0) // CHUNK
    c = lax.broadcasted_iota(jnp.int32, (n, n), 1) // CHUNK
    return c <= r


def _mla_attn_kernel(q_ref, kn_ref, kr_ref, v_ref, o_ref):
    seq = q_ref.shape[0]
    k = jnp.concatenate([kn_ref[...], kr_ref[...]], axis=1)
    mask = _chunk_causal_mask(ATT_TQ)
    for i in range(seq // ATT_TQ):
        lo = i * ATT_TQ
        q = q_ref[lo:lo + ATT_TQ, :]
        s_diag = jnp.where(mask, _dot_nt(q, k[lo:lo + ATT_TQ]), -jnp.inf)
        s_parts, v_parts = [s_diag], [v_ref[lo:lo + ATT_TQ, :]]
        if i > 0:
            s_parts.append(_dot_nt(q, k[:lo]))
            v_parts.append(v_ref[:lo, :])
        o_ref[lo:lo + ATT_TQ, :] = _softmax_pv(s_parts, v_parts).astype(BF16)


def _mla_attention(q, kn, kr, v):
    bsz, seq, _ = q.shape
    head = lambda b, h: (b, 0, h)
    return pl.pallas_call(
        _mla_attn_kernel,
        grid=(bsz, A_HEADS),
        in_specs=[
            pl.BlockSpec((None, seq, A_QBLK), head),
            pl.BlockSpec((None, seq, A_NOPE), head),
            pl.BlockSpec((None, seq, 2 * A_ROPE), lambda b, h: (b, 0, 0)),
            pl.BlockSpec((None, seq, A_VDIM), head),
        ],
        out_specs=pl.BlockSpec((None, seq, A_VDIM), head),
        out_shape=jax.ShapeDtypeStruct((bsz, seq, A_WIDTH), BF16),
        compiler_params=_params("arbitrary", "arbitrary"),
        name="mla_attention",
    )(q, kn, kr, v)


def _band_attn_kernel(q_ref, k_ref, v_ref, e_ref, o_ref, bias_ref):
    seq = q_ref.shape[0]
    scale = float(B_HDIM) ** -0.5

    @pl.when(pl.program_id(1) == 0)
    def _():
        e = jnp.broadcast_to(e_ref[...], (ATT_TQ, e_ref.shape[1]))
        toep = pltpu.roll(e, 0, 1, stride=1, stride_axis=0)[:, :BAND_KEYS]
        qc = lax.broadcasted_iota(jnp.int32, toep.shape, 0) // CHUNK
        kc = lax.broadcasted_iota(jnp.int32, toep.shape, 1) // CHUNK - B_PREV
        bias_ref[...] = jnp.where((kc <= qc) & (kc >= qc - B_PREV), toep, -jnp.inf)

    for i in range(seq // ATT_TQ):
        lo = i * ATT_TQ
        klo = max(0, lo - B_PREV * CHUNK)
        nk = lo + ATT_TQ - klo
        q = (q_ref[lo:lo + ATT_TQ, :].astype(F32) * scale).astype(BF16)
        s = _dot_nt(q, k_ref[klo:klo + nk, :]) + bias_ref[:, BAND_KEYS - nk:]
        o_ref[lo:lo + ATT_TQ, :] = _softmax_pv([s], [v_ref[klo:klo + nk, :]]).astype(BF16)


def _band_attention(z3, e_tab):
    bsz, seq, _ = z3.shape
    nb = LATENT_W // B_HDIM

    def col(off):
        return pl.BlockSpec((None, seq, B_HDIM), lambda h, b: (b, 0, off + h))

    return pl.pallas_call(
        _band_attn_kernel,
        grid=(B_HEADS, bsz),
        in_specs=[
            col(nb), col(nb + B_HEADS), col(nb + 2 * B_HEADS),
            pl.BlockSpec((None, 1, e_tab.shape[2]), lambda h, b: (h, 0, 0)),
        ],
        out_specs=pl.BlockSpec((None, seq, B_HDIM), lambda h, b: (b, 0, h)),
        out_shape=jax.ShapeDtypeStruct((bsz, seq, B_WIDTH), BF16),
        scratch_shapes=[pltpu.VMEM((ATT_TQ, BAND_KEYS), F32)],
        compiler_params=_params("arbitrary", "arbitrary"),
        name="band_attention",
    )(z3, z3, z3, e_tab)


def _band_bias_row(rel_table):
    h = rel_table.shape[0]
    width = 2 * (B_PREV * CHUNK)
    n_hi = B_PREV * CHUNK - REL_CLIP
    n_lo = width - n_hi - (2 * REL_CLIP + 1) - (CHUNK - 1)
    hi = rel_table[:, 2 * REL_CLIP:]
    lo = rel_table[:, :1]
    e = jnp.concatenate([jnp.broadcast_to(hi, (h, n_hi)), rel_table[:, ::-1],
                         jnp.broadcast_to(lo, (h, n_lo)),
                         jnp.broadcast_to(hi, (h, CHUNK - 1))], axis=1)
    return e.astype(F32)[:, None, :]


def _finish(y_fn, refs, emit_h):
    x_ref, w_ref, gpost_ref, mgate_ref = refs[:4]
    o_ref = refs[7 if emit_h else 4]
    wb_ref, r0_ref, r1_ref = refs[-3:]
    i = pl.program_id(0)

    @pl.when(i == 0)
    def _():
        wb_ref[...] = w_ref[...].astype(BF16)
        r1_ref[...] = jnp.zeros_like(r1_ref)

    def step(r_prev, r_cur):
        xn = x_ref[...] + mgate_ref[...] * (_rms(r_prev[...]) * gpost_ref[...])
        o_ref[...] = xn
        if emit_h:
            refs[8][...] = _adaln(xn, *refs[4:7])
        r_cur[...] = _dot(y_fn().astype(BF16), wb_ref[...])

    pl.when(i % 2 == 0)(lambda: step(r1_ref, r0_ref))
    pl.when(i % 2 == 1)(lambda: step(r0_ref, r1_ref))


def _finish_specs(x2, w_stack, layer, g_post, mgate, nxt, seq, tm):
    t, d = x2.shape
    per_b = seq // tm
    prev = lambda i: jnp.maximum(i - 1, 0)
    row = pl.BlockSpec((tm, d), lambda i: (prev(i), 0))
    vec = pl.BlockSpec((1, d), lambda i: (0, 0))
    bvec = pl.BlockSpec((None, 1, d), lambda i: (prev(i) // per_b, 0, 0))
    whole = pl.BlockSpec((None, d, d), lambda i: (layer, 0, 0), pipeline_mode=pl.Buffered(1))
    args, in_specs = [x2, w_stack, g_post, mgate], [row, whole, vec, bvec]
    out_specs, out_shape = [row], [jax.ShapeDtypeStruct((t, d), F32)]
    if nxt is not None:
        args += list(nxt)
        in_specs += [vec, bvec, bvec]
        out_specs.append(row)
        out_shape.append(jax.ShapeDtypeStruct((t, d), BF16))
    scratch = [pltpu.VMEM((d, d), BF16), pltpu.VMEM((tm, d), F32), pltpu.VMEM((tm, d), F32)]
    return args, in_specs, out_specs, out_shape, scratch


def _outproj_even_kernel(emit_h, oa_ref, ob_ref, gate_ref, *refs):
    def y_fn():
        o = jnp.concatenate([oa_ref[...], ob_ref[...]], axis=1).astype(F32)
        return o * _silu(gate_ref[...].astype(F32))

    _finish(y_fn, refs, emit_h)


def _outproj_even(oa, ob, z, x2, w_stack, layer, g_post, mgate, nxt, seq, tm=256):
    t, d = x2.shape
    n_tiles = t // tm
    cur = lambda i: jnp.minimum(i, n_tiles - 1)
    args, in_specs, out_specs, out_shape, scratch = _finish_specs(
        x2, w_stack, layer, g_post, mgate, nxt, seq, tm)
    return pl.pallas_call(
        functools.partial(_outproj_even_kernel, nxt is not None),
        grid=(n_tiles + 1,),
        in_specs=[
            pl.BlockSpec((tm, A_WIDTH), lambda i: (cur(i), 0)),
            pl.BlockSpec((tm, B_WIDTH), lambda i: (cur(i), 0)),
            pl.BlockSpec((tm, d), lambda i: (cur(i), 2)),
        ] + in_specs,
        out_specs=out_specs,
        out_shape=out_shape,
        scratch_shapes=scratch,
        compiler_params=_params("arbitrary"),
        name="outproj_even",
    )(oa, ob, z, *args)


def _sgu_kernel(emit_h, u_ref, v_ref, gate_ref, lng_ref, lnb_ref, ws_ref, bs_ref, *refs):
    sv_ref, refs = refs[-1], refs[:-1]
    tm, d = sv_ref.shape
    dg = d // SG_GROUPS

    def y_fn():
        v = v_ref[...].astype(F32)
        mu = jnp.mean(v, axis=-1, keepdims=True)
        vc = v - mu
        var = jnp.mean(vc * vc, axis=-1, keepdims=True)
        vn = ((vc * lax.rsqrt(var + EPS)) * lng_ref[...] + lnb_ref[...]).astype(BF16)
        cpos_r = lax.broadcasted_iota(jnp.int32, (SG_LEN, SG_LEN), 0) // CHUNK
        cpos_c = lax.broadcasted_iota(jnp.int32, (SG_LEN, SG_LEN), 1) // CHUNK
        mask = cpos_c <= cpos_r
        for g in range(SG_GROUPS):
            ws = jnp.where(mask, ws_ref[g], 0.0).astype(BF16)
            bs = bs_ref[:, g:g + 1]
            for n in range(tm // SG_LEN):
                blk = vn[n * SG_LEN:(n + 1) * SG_LEN, g * dg:(g + 1) * dg]
                sv_ref[n * SG_LEN:(n + 1) * SG_LEN, g * dg:(g + 1) * dg] = _dot(ws, blk) + bs
        return u_ref[...].astype(F32) * sv_ref[...] * _silu(gate_ref[...].astype(F32))

    _finish(y_fn, refs, emit_h)


def _sgu_outproj(z, x2, ln_g, ln_b, w_s, b_s_t, w_stack, layer, g_post, mgate, nxt, seq, tm=256):
    t, d = x2.shape
    n_tiles = t // tm
    cur = lambda i: jnp.minimum(i, n_tiles - 1)
    const2 = lambda i: (0, 0)
    args, in_specs, out_specs, out_shape, scratch = _finish_specs(
        x2, w_stack, layer, g_post, mgate, nxt, seq, tm)
    return pl.pallas_call(
        functools.partial(_sgu_kernel, nxt is not None),
        grid=(n_tiles + 1,),
        in_specs=[
            pl.BlockSpec((tm, d), lambda i: (cur(i), 0)),
            pl.BlockSpec((tm, d), lambda i: (cur(i), 1)),
            pl.BlockSpec((tm, d), lambda i: (cur(i), 2)),
            pl.BlockSpec((1, d), const2),
            pl.BlockSpec((1, d), const2),
            pl.BlockSpec(w_s.shape, lambda i: (0, 0, 0)),
            pl.BlockSpec(b_s_t.shape, const2),
        ] + in_specs,
        out_specs=out_specs,
        out_shape=out_shape,
        scratch_shapes=scratch + [pltpu.VMEM((tm, d), F32)],
        compiler_params=_params("arbitrary"),
        name="sgu_outproj",
    )(z, z, z, ln_g, ln_b, w_s, b_s_t, *args)


def _half_swap(w):
    half = w.shape[-1] // 2
    return jnp.concatenate([-w[..., half:], w[..., :half]], axis=-1)


def _prep_latent_weights(w_uq, w_ukv):
    wq = w_uq.reshape(A_QLORA, A_HEADS, A_NOPE + A_ROPE)
    wq_rope = wq[..., A_NOPE:]
    wq_p = jnp.concatenate([wq, _half_swap(wq_rope)], axis=-1).reshape(A_QLORA, A_HEADS * A_QBLK)
    wkv = w_ukv.reshape(A_KVLORA, A_HEADS, A_NOPE + A_VDIM)
    wkv_p = jnp.concatenate([wkv[..., :A_NOPE].reshape(A_KVLORA, A_WIDTH),
                             wkv[..., A_NOPE:].reshape(A_KVLORA, A_WIDTH)], axis=1)
    return wq_p.astype(BF16), wkv_p.astype(BF16)


def _rope_table(seq):
    half = A_ROPE // 2
    pos = jnp.arange(seq, dtype=F32)
    freqs = ROPE_THETA ** (-jnp.arange(half, dtype=F32) / half)
    ang = pos[:, None] * freqs[None, :]
    cos, sin = jnp.cos(ang), jnp.sin(ang)
    return jnp.concatenate([cos, cos, sin, sin], axis=1)


def kernel(x, c, w_mod, b_mod, g_pre, g_post, ab_w_in, a_g_q, a_w_uq, a_g_kv, a_w_ukv,
           b_rel_bias, ab_w_out, sg_w_in, sg_ln_g, sg_ln_b, sg_w_s, sg_b_s, sg_w_out):
    bsz, seq, d = x.shape
    t = bsz * seq
    mod = _modulation(c, w_mod, b_mod)
    cs_tab = _rope_table(seq)
    x2 = x.reshape(t, d)

    def adaln_args(l):
        return g_pre[l][None], mod[l, :, None, d:2 * d], mod[l, :, None, :d]

    ab_w_in_t = jnp.swapaxes(ab_w_in, 1, 2)
    h = _prenorm(x2, *adaln_args(0), seq)
    for l in range(DEPTH):
        mgate = mod[l, :, None, 2 * d:]
        nxt = adaln_args(l + 1) if l + 1 < DEPTH else None
        i = l // 2
        if l % 2 == 0:
            w_uq, w_ukv = _prep_latent_weights(a_w_uq[i], a_w_ukv[i])
            z = _inproj(h, ab_w_in_t, i, Z_W, transposed=True,
                        carry=LATENT_W - (A_QLORA + A_KVLORA + A_ROPE))
            q, kn, v, kr = _latent(z, a_g_q[i][None], w_uq, a_g_kv[i][None], w_ukv, cs_tab, seq)
            oa = _mla_attention(q.reshape(bsz, seq, -1), kn.reshape(bsz, seq, -1),
                                kr.reshape(bsz, seq, -1), v.reshape(bsz, seq, -1))
            ob = _band_attention(z.reshape(bsz, seq, Z_W), _band_bias_row(b_rel_bias[i]))
            out = _outproj_even(oa.reshape(t, A_WIDTH), ob.reshape(t, B_WIDTH), z, x2,
                                ab_w_out, i, g_post[l][None], mgate, nxt, seq)
        else:
            z = _inproj(h, sg_w_in, i, sg_w_in.shape[2])
            out = _sgu_outproj(z, x2, sg_ln_g[i][None], sg_ln_b[i][None], sg_w_s[i],
                               sg_b_s[i].T, sg_w_out, i, g_post[l][None], mgate, nxt, seq)
        x2, h = out if nxt is not None else (out[0], None)
    return x2.reshape(bsz, seq, d)
```

```python
import functools

import jax
import jax.numpy as jnp
from jax import lax
from jax.experimental import pallas as pl
from jax.experimental.pallas import tpu as pltpu

D_MODEL = 2048
DEPTH = 4
CHUNK = 64
EPS = 1e-6
A_HEADS = 8
A_NOPE = 128
A_ROPE = 64
A_VDIM = 128
A_QLORA = 512
A_KVLORA = 256
ROPE_THETA = 10000.0
B_HEADS = 8
B_HDIM = 128
B_PREV = 8
REL_CLIP = 128
SG_GROUPS = 8
SG_LEN = 128
B_WIDTH = B_HEADS * B_HDIM
A_WIDTH = A_HEADS * A_VDIM

VMEM_LIMIT_BYTES = 56 * 1024 * 1024

SUBLANES = 8

LATENT_W = 1024
Z_W = 6144
A_QBLK = 256
ATT_TQ = 256
BAND_KEYS = ATT_TQ + B_PREV * CHUNK

BF16 = jnp.bfloat16
F32 = jnp.float32
LOG2E = 1.4426950408889634


def _params(*sem):
    return pltpu.CompilerParams(dimension_semantics=sem, vmem_limit_bytes=VMEM_LIMIT_BYTES)


def _dot(a, b):
    return jnp.dot(a, b, preferred_element_type=F32)


def _dot_nt(a, b):
    return lax.dot_general(a, b, (((1,), (1,)), ((), ())), preferred_element_type=F32)


def _rms(x):
    return x * lax.rsqrt(jnp.mean(x * x, axis=-1, keepdims=True) + EPS)


def _silu(x):
    return x * jax.nn.sigmoid(x)


def _mod_kernel(c_ref, w_ref, b_ref, o_ref):
    cs = _silu(c_ref[...]).astype(BF16)
    o_ref[...] = _dot(cs, w_ref[...].astype(BF16)) + b_ref[...]


def _modulation(c, w_mod, b_mod, tn=512):
    depth, d, n = w_mod.shape
    bsz = c.shape[0]
    return pl.pallas_call(
        _mod_kernel,
        grid=(depth, n // tn),
        in_specs=[
            pl.BlockSpec((bsz, d), lambda l, j: (0, 0)),
            pl.BlockSpec((None, d, tn), lambda l, j: (l, 0, j)),
            pl.BlockSpec((None, 1, tn), lambda l, j: (l, 0, j)),
        ],
        out_specs=pl.BlockSpec((None, bsz, tn), lambda l, j: (l, 0, j)),
        out_shape=jax.ShapeDtypeStruct((depth, bsz, n), F32),
        compiler_params=_params("arbitrary", "arbitrary"),
        name="modulation",
    )(c, w_mod, b_mod.reshape(depth, 1, n))


def _adaln(x, g_ref, scale_ref, shift_ref):
    return (_rms(x) * g_ref[...] * (1.0 + scale_ref[...]) + shift_ref[...]).astype(BF16)


def _prenorm_kernel(x_ref, g_ref, scale_ref, shift_ref, h_ref):
    h_ref[...] = _adaln(x_ref[...], g_ref, scale_ref, shift_ref)


def _prenorm(x2, g_pre, scale, shift, seq, tm=512):
    t, d = x2.shape
    per_b = seq // tm
    vec = pl.BlockSpec((None, 1, d), lambda m: (m // per_b, 0, 0))
    return pl.pallas_call(
        _prenorm_kernel,
        grid=(t // tm,),
        in_specs=[pl.BlockSpec((tm, d), lambda m: (m, 0)),
                  pl.BlockSpec((1, d), lambda m: (0, 0)), vec, vec],
        out_specs=pl.BlockSpec((tm, d), lambda m: (m, 0)),
        out_shape=jax.ShapeDtypeStruct((t, d), BF16),
        compiler_params=_params("arbitrary"),
        name="prenorm",
    )(x2, g_pre, scale, shift)


def _inproj_kernel(transposed, h_ref, w_ref, o_ref, wb_ref):
    @pl.when(pl.program_id(1) == 0)
    def _():
        wb_ref[...] = (w_ref[0].T if transposed else w_ref[...]).astype(BF16)

    o_ref[...] = _dot(h_ref[...], wb_ref[...]).astype(BF16)


def _inproj(h, w_stack, layer, n_out, transposed=False, carry=0, tm=1024, tn=1024):
    t, d = h.shape
    if transposed:
        w_spec = pl.BlockSpec(
            (pl.Element(1), pl.Element(tn), pl.Element(d)),
            lambda n, m: (layer, pl.multiple_of(jnp.maximum(n * tn - carry, 0), SUBLANES), 0))
    else:
        assert carry == 0
        w_spec = pl.BlockSpec((None, d, tn), lambda n, m: (layer, 0, n))
    return pl.pallas_call(
        functools.partial(_inproj_kernel, transposed),
        grid=(n_out // tn, t // tm),
        in_specs=[pl.BlockSpec((tm, d), lambda n, m: (m, 0)), w_spec],
        out_specs=pl.BlockSpec((tm, tn), lambda n, m: (m, n)),
        out_shape=jax.ShapeDtypeStruct((t, n_out), BF16),
        scratch_shapes=[pltpu.VMEM((d, tn), BF16)],
        compiler_params=_params("arbitrary", "arbitrary"),
        name="inproj",
    )(h, w_stack)


def _latent_kernel(z_ref, gq_ref, wq_ref, gkv_ref, wkv_ref, cs_ref,
                   q_ref, kn_ref, v_ref, kr_ref):
    z = z_ref[...].astype(F32)
    cs = cs_ref[...]
    cq = (_rms(z[:, :A_QLORA]) * gq_ref[...]).astype(BF16)
    q = _dot(cq, wq_ref[...])
    qscale = LOG2E * float(A_NOPE + A_ROPE) ** -0.5
    for h in range(A_HEADS):
        lo = h * A_QBLK
        q_ref[:, lo:lo + A_NOPE] = (q[:, lo:lo + A_NOPE] * qscale).astype(BF16)
        pr = q[:, lo + A_NOPE:lo + A_QBLK] * cs
        pr = pr + pltpu.roll(pr, A_ROPE, 1)
        q_ref[:, lo + A_NOPE:lo + A_QBLK] = (pr * qscale).astype(BF16)
    ckv = (_rms(z[:, A_QLORA:A_QLORA + A_KVLORA]) * gkv_ref[...]).astype(BF16)
    kv = _dot(ckv, wkv_ref[...])
    kn_ref[...] = kv[:, :A_WIDTH].astype(BF16)
    v_ref[...] = kv[:, A_WIDTH:].astype(BF16)
    o = A_QLORA + A_KVLORA
    kr = z[:, o:o + 2 * A_ROPE]
    lane = lax.broadcasted_iota(jnp.int32, kr.shape, 1)
    half = A_ROPE // 2
    swapped = jnp.where(lane < half, -pltpu.roll(kr, 2 * A_ROPE - half, 1), pltpu.roll(kr, half, 1))
    roped = kr * cs + swapped * pltpu.roll(cs, A_ROPE, 1)
    kr_ref[...] = jnp.where(lane < A_ROPE, roped, 0.0).astype(BF16)


def _latent(z, g_q, w_uq, g_kv, w_ukv, cs_tab, seq, tm=512):
    t = z.shape[0]
    per_b = seq // tm
    const = lambda m: (0, 0)
    return pl.pallas_call(
        _latent_kernel,
        grid=(t // tm,),
        in_specs=[
            pl.BlockSpec((tm, LATENT_W), lambda m: (m, 0)),
            pl.BlockSpec((1, A_QLORA), const),
            pl.BlockSpec(w_uq.shape, const),
            pl.BlockSpec((1, A_KVLORA), const),
            pl.BlockSpec(w_ukv.shape, const),
            pl.BlockSpec((tm, 2 * A_ROPE), lambda m: (m % per_b, 0)),
        ],
        out_specs=[
            pl.BlockSpec((tm, A_HEADS * A_QBLK), lambda m: (m, 0)),
            pl.BlockSpec((tm, A_WIDTH), lambda m: (m, 0)),
            pl.BlockSpec((tm, A_WIDTH), lambda m: (m, 0)),
            pl.BlockSpec((tm, 2 * A_ROPE), lambda m: (m, 0)),
        ],
        out_shape=[
            jax.ShapeDtypeStruct((t, A_HEADS * A_QBLK), BF16),
            jax.ShapeDtypeStruct((t, A_WIDTH), BF16),
            jax.ShapeDtypeStruct((t, A_WIDTH), BF16),
            jax.ShapeDtypeStruct((t, 2 * A_ROPE), BF16),
        ],
        compiler_params=_params("arbitrary"),
        name="latent",
    )(z, g_q, w_uq, g_kv, w_ukv, cs_tab)


def _with_ones_column(v):
    lane = lax.broadcasted_iota(jnp.int32, v.shape, 1)
    return jnp.concatenate([v, jnp.where(lane == 0, 1.0, 0.0).astype(v.dtype)], axis=1)


def _softmax_pv(s_parts, v_parts):
    m = functools.reduce(jnp.maximum, [jnp.max(s, axis=-1, keepdims=True) for s in s_parts])
    acc = functools.reduce(
        jnp.add, [_dot(jnp.exp2(s - m).astype(BF16), v) for s, v in zip(s_parts, v_parts)])
    width = acc.shape[1] // 2
    return acc[:, :width] / acc[:, width:width + 1]


def _chunk_causal_mask(n):
    r = lax.broadcasted_iota(jnp.int32, (n, n), 0) // CHUNK
    c = lax.broadcasted_iota(jnp.int32, (n, n), 1) // CHUNK
    return c <= r


def _mla_attn_kernel(q_ref, kn_ref, kr_ref, v_ref, o_ref):
    seq = q_ref.shape[0]
    k = jnp.concatenate([kn_ref[...], kr_ref[...]], axis=1)
    v = _with_ones_column(v_ref[...])
    mask = _chunk_causal_mask(ATT_TQ)
    for i in reversed(range(seq // ATT_TQ)):
        lo = i * ATT_TQ
        q = q_ref[lo:lo + ATT_TQ, :]
        s_diag = jnp.where(mask, _dot_nt(q, k[lo:lo + ATT_TQ]), -jnp.inf)
        s_parts, v_parts = [s_diag], [v[lo:lo + ATT_TQ]]
        if i > 0:
            s_parts.append(_dot_nt(q, k[:lo]))
            v_parts.append(v[:lo])
        o_ref[lo:lo + ATT_TQ, :] = _softmax_pv(s_parts, v_parts).astype(BF16)


def _mla_attention(q, kn, kr, v):
    bsz, seq, _ = q.shape
    head = lambda b, h: (b, 0, h)
    return pl.pallas_call(
        _mla_attn_kernel,
        grid=(bsz, A_HEADS),
        in_specs=[
            pl.BlockSpec((None, seq, A_QBLK), head),
            pl.BlockSpec((None, seq, A_NOPE), head),
            pl.BlockSpec((None, seq, 2 * A_ROPE), lambda b, h: (b, 0, 0)),
            pl.BlockSpec((None, seq, A_VDIM), head),
        ],
        out_specs=pl.BlockSpec((None, seq, A_VDIM), head),
        out_shape=jax.ShapeDtypeStruct((bsz, seq, A_WIDTH), BF16),
        compiler_params=_params("arbitrary", "arbitrary"),
        name="mla_attention",
    )(q, kn, kr, v)


def _band_attn_kernel(q_ref, k_ref, v_ref, e_ref, o_ref, bias_ref):
    seq = q_ref.shape[0]
    scale = LOG2E * float(B_HDIM) ** -0.5

    @pl.when(pl.program_id(1) == 0)
    def _():
        e = jnp.broadcast_to(e_ref[...] * LOG2E, (ATT_TQ, e_ref.shape[1]))
        toep = pltpu.roll(e, 0, 1, stride=1, stride_axis=0)[:, :BAND_KEYS]
        qc = lax.broadcasted_iota(jnp.int32, toep.shape, 0) // CHUNK
        kc = lax.broadcasted_iota(jnp.int32, toep.shape, 1) // CHUNK - B_PREV
        bias_ref[...] = jnp.where((kc <= qc) & (kc >= qc - B_PREV), toep, -jnp.inf)

    v = _with_ones_column(v_ref[...])
    for i in reversed(range(seq // ATT_TQ)):
        lo = i * ATT_TQ
        klo = max(0, lo - B_PREV * CHUNK)
        nk = lo + ATT_TQ - klo
        q = (q_ref[lo:lo + ATT_TQ, :].astype(F32) * scale).astype(BF16)
        s = _dot_nt(q, k_ref[klo:klo + nk, :]) + bias_ref[:, BAND_KEYS - nk:]
        o_ref[lo:lo + ATT_TQ, :] = _softmax_pv([s], [v[klo:klo + nk]]).astype(BF16)


def _band_attention(z3, e_tab):
    bsz, seq, _ = z3.shape
    nb = LATENT_W // B_HDIM

    def col(off):
        return pl.BlockSpec((None, seq, B_HDIM), lambda h, b: (b, 0, off + h))

    return pl.pallas_call(
        _band_attn_kernel,
        grid=(B_HEADS, bsz),
        in_specs=[
            col(nb), col(nb + B_HEADS), col(nb + 2 * B_HEADS),
            pl.BlockSpec((None, 1, e_tab.shape[2]), lambda h, b: (h, 0, 0)),
        ],
        out_specs=pl.BlockSpec((None, seq, B_HDIM), lambda h, b: (b, 0, h)),
        out_shape=jax.ShapeDtypeStruct((bsz, seq, B_WIDTH), BF16),
        scratch_shapes=[pltpu.VMEM((ATT_TQ, BAND_KEYS), F32)],
        compiler_params=_params("arbitrary", "arbitrary"),
        name="band_attention",
    )(z3, z3, z3, e_tab)


def _band_bias_row(rel_table):
    h = rel_table.shape[0]
    width = 2 * (B_PREV * CHUNK)
    n_hi = B_PREV * CHUNK - REL_CLIP
    n_lo = width - n_hi - (2 * REL_CLIP + 1) - (CHUNK - 1)
    hi = rel_table[:, 2 * REL_CLIP:]
    lo = rel_table[:, :1]
    e = jnp.concatenate([jnp.broadcast_to(hi, (h, n_hi)), rel_table[:, ::-1],
                         jnp.broadcast_to(lo, (h, n_lo)),
                         jnp.broadcast_to(hi, (h, CHUNK - 1))], axis=1)
    return e.astype(F32)[:, None, :]


def _finish(y_fn, refs, emit_h):
    x_ref, w_ref, gpost_ref, mgate_ref = refs[:4]
    o_ref = refs[7 if emit_h else 4]
    wb_ref, r0_ref, r1_ref = refs[-3:]
    i = pl.program_id(0)

    @pl.when(i == 0)
    def _():
        wb_ref[...] = w_ref[...].astype(BF16)
        r1_ref[...] = jnp.zeros_like(r1_ref)

    def step(r_prev, r_cur):
        xn = x_ref[...] + mgate_ref[...] * (_rms(r_prev[...]) * gpost_ref[...])
        o_ref[...] = xn
        if emit_h:
            refs[8][...] = _adaln(xn, *refs[4:7])
        r_cur[...] = _dot(y_fn().astype(BF16), wb_ref[...])

    pl.when(i % 2 == 0)(lambda: step(r1_ref, r0_ref))
    pl.when(i % 2 == 1)(lambda: step(r0_ref, r1_ref))


def _finish_specs(x2, w_stack, layer, g_post, mgate, nxt, seq, tm):
    t, d = x2.shape
    per_b = seq // tm
    prev = lambda i: jnp.maximum(i - 1, 0)
    row = pl.BlockSpec((tm, d), lambda i: (prev(i), 0))
    vec = pl.BlockSpec((1, d), lambda i: (0, 0))
    bvec = pl.BlockSpec((None, 1, d), lambda i: (prev(i) // per_b, 0, 0))
    whole = pl.BlockSpec((None, d, d), lambda i: (layer, 0, 0), pipeline_mode=pl.Buffered(1))
    args, in_specs = [x2, w_stack, g_post, mgate], [row, whole, vec, bvec]
    out_specs, out_shape = [row], [jax.ShapeDtypeStruct((t, d), F32)]
    if nxt is not None:
        args += list(nxt)
        in_specs += [vec, bvec, bvec]
        out_specs.append(row)
        out_shape.append(jax.ShapeDtypeStruct((t, d), BF16))
    scratch = [pltpu.VMEM((d, d), BF16), pltpu.VMEM((tm, d), F32), pltpu.VMEM((tm, d), F32)]
    return args, in_specs, out_specs, out_shape, scratch


def _outproj_even_kernel(emit_h, oa_ref, ob_ref, gate_ref, *refs):
    def y_fn():
        o = jnp.concatenate([oa_ref[...], ob_ref[...]], axis=1).astype(F32)
        return o * _silu(gate_ref[...].astype(F32))

    _finish(y_fn, refs, emit_h)


def _outproj_even(oa, ob, z, x2, w_stack, layer, g_post, mgate, nxt, seq, tm=256):
    t, d = x2.shape
    n_tiles = t // tm
    cur = lambda i: jnp.minimum(i, n_tiles - 1)
    args, in_specs, out_specs, out_shape, scratch = _finish_specs(
        x2, w_stack, layer, g_post, mgate, nxt, seq, tm)
    return pl.pallas_call(
        functools.partial(_outproj_even_kernel, nxt is not None),
        grid=(n_tiles + 1,),
        in_specs=[
            pl.BlockSpec((tm, A_WIDTH), lambda i: (cur(i), 0)),
            pl.BlockSpec((tm, B_WIDTH), lambda i: (cur(i), 0)),
            pl.BlockSpec((tm, d), lambda i: (cur(i), 2)),
        ] + in_specs,
        out_specs=out_specs,
        out_shape=out_shape,
        scratch_shapes=scratch,
        compiler_params=_params("arbitrary"),
        name="outproj_even",
    )(oa, ob, z, *args)


def _sgu_kernel(emit_h, u_ref, v_ref, gate_ref, lng_ref, lnb_ref, ws_ref, bs_ref, *refs):
    sv_ref, refs = refs[-1], refs[:-1]
    tm, d = sv_ref.shape
    dg = d // SG_GROUPS

    def y_fn():
        v = v_ref[...].astype(F32)
        mu = jnp.mean(v, axis=-1, keepdims=True)
        vc = v - mu
        var = jnp.mean(vc * vc, axis=-1, keepdims=True)
        vn = ((vc * lax.rsqrt(var + EPS)) * lng_ref[...] + lnb_ref[...]).astype(BF16)
        cpos_r = lax.broadcasted_iota(jnp.int32, (SG_LEN, SG_LEN), 0) // CHUNK
        cpos_c = lax.broadcasted_iota(jnp.int32, (SG_LEN, SG_LEN), 1) // CHUNK
        mask = cpos_c <= cpos_r
        for g in range(SG_GROUPS):
            ws = jnp.where(mask, ws_ref[g], 0.0).astype(BF16)
            bs = bs_ref[:, g:g + 1]
            for n in range(tm // SG_LEN):
                blk = vn[n * SG_LEN:(n + 1) * SG_LEN, g * dg:(g + 1) * dg]
                sv_ref[n * SG_LEN:(n + 1) * SG_LEN, g * dg:(g + 1) * dg] = _dot(ws, blk) + bs
        return u_ref[...].astype(F32) * sv_ref[...] * _silu(gate_ref[...].astype(F32))

    _finish(y_fn, refs, emit_h)


def _sgu_outproj(z, x2, ln_g, ln_b, w_s, b_s_t, w_stack, layer, g_post, mgate, nxt, seq, tm=256):
    t, d = x2.shape
    n_tiles = t // tm
    cur = lambda i: jnp.minimum(i, n_tiles - 1)
    const2 = lambda i: (0, 0)
    args, in_specs, out_specs, out_shape, scratch = _finish_specs(
        x2, w_stack, layer, g_post, mgate, nxt, seq, tm)
    return pl.pallas_call(
        functools.partial(_sgu_kernel, nxt is not None),
        grid=(n_tiles + 1,),
        in_specs=[
            pl.BlockSpec((tm, d), lambda i: (cur(i), 0)),
            pl.BlockSpec((tm, d), lambda i: (cur(i), 1)),
            pl.BlockSpec((tm, d), lambda i: (cur(i), 2)),
            pl.BlockSpec((1, d), const2),
            pl.BlockSpec((1, d), const2),
            pl.BlockSpec(w_s.shape, lambda i: (0, 0, 0)),
            pl.BlockSpec(b_s_t.shape, const2),
        ] + in_specs,
        out_specs=out_specs,
        out_shape=out_shape,
        scratch_shapes=scratch + [pltpu.VMEM((tm, d), F32)],
        compiler_params=_params("arbitrary"),
        name="sgu_outproj",
    )(z, z, z, ln_g, ln_b, w_s, b_s_t, *args)


def _half_swap(w):
    half = w.shape[-1] // 2
    return jnp.concatenate([-w[..., half:], w[..., :half]], axis=-1)


def _prep_latent_weights(w_uq, w_ukv):
    wq = w_uq.reshape(A_QLORA, A_HEADS, A_NOPE + A_ROPE)
    wq_rope = wq[..., A_NOPE:]
    wq_p = jnp.concatenate([wq, _half_swap(wq_rope)], axis=-1).reshape(A_QLORA, A_HEADS * A_QBLK)
    wkv = w_ukv.reshape(A_KVLORA, A_HEADS, A_NOPE + A_VDIM)
    wkv_p = jnp.concatenate([wkv[..., :A_NOPE].reshape(A_KVLORA, A_WIDTH),
                             wkv[..., A_NOPE:].reshape(A_KVLORA, A_WIDTH)], axis=1)
    return wq_p.astype(BF16), wkv_p.astype(BF16)


def _rope_table(seq):
    half = A_ROPE // 2
    pos = jnp.arange(seq, dtype=F32)
    freqs = ROPE_THETA ** (-jnp.arange(half, dtype=F32) / half)
    ang = pos[:, None] * freqs[None, :]
    cos, sin = jnp.cos(ang), jnp.sin(ang)
    return jnp.concatenate([cos, cos, sin, sin], axis=1)


def kernel(x, c, w_mod, b_mod, g_pre, g_post, ab_w_in, a_g_q, a_w_uq, a_g_kv, a_w_ukv,
           b_rel_bias, ab_w_out, sg_w_in, sg_ln_g, sg_ln_b, sg_w_s, sg_b_s, sg_w_out):
    bsz, seq, d = x.shape
    t = bsz * seq
    mod = _modulation(c, w_mod, b_mod)
    cs_tab = _rope_table(seq)
    x2 = x.reshape(t, d)

    def adaln_args(l):
        return g_pre[l][None], mod[l, :, None, d:2 * d], mod[l, :, None, :d]

    ab_w_in_t = jnp.swapaxes(ab_w_in, 1, 2)
    h = _prenorm(x2, *adaln_args(0), seq)
    for l in range(DEPTH):
        mgate = mod[l, :, None, 2 * d:]
        nxt = adaln_args(l + 1) if l + 1 < DEPTH else None
        i = l // 2
        if l % 2 == 0:
            w_uq, w_ukv = _prep_latent_weights(a_w_uq[i], a_w_ukv[i])
            z = _inproj(h, ab_w_in_t, i, Z_W, transposed=True,
                        carry=LATENT_W - (A_QLORA + A_KVLORA + A_ROPE))
            q, kn, v, kr = _latent(z, a_g_q[i][None], w_uq, a_g_kv[i][None], w_ukv, cs_tab, seq)
            oa = _mla_attention(q.reshape(bsz, seq, -1), kn.reshape(bsz, seq, -1),
                                kr.reshape(bsz, seq, -1), v.reshape(bsz, seq, -1))
            ob = _band_attention(z.reshape(bsz, seq, Z_W), _band_bias_row(b_rel_bias[i]))
            out = _outproj_even(oa.reshape(t, A_WIDTH), ob.reshape(t, B_WIDTH), z, x2,
                                ab_w_out, i, g_post[l][None], mgate, nxt, seq)
        else:
            z = _inproj(h, sg_w_in, i, sg_w_in.shape[2])
            out = _sgu_outproj(z, x2, sg_ln_g[i][None], sg_ln_b[i][None], sg_w_s[i],
                               sg_b_s[i].T, sg_w_out, i, g_post[l][None], mgate, nxt, seq)
        x2, h = out if nxt is not None else (out[0], None)
    return x2.reshape(bsz, seq, d)
```

```python
import functools

import jax
import jax.numpy as jnp
from jax import lax
from jax.experimental import pallas as pl
from jax.experimental.pallas import tpu as pltpu

D_MODEL = 2048
DEPTH = 4
CHUNK = 64
EPS = 1e-6
A_HEADS = 8
A_NOPE = 128
A_ROPE = 64
A_VDIM = 128
A_QLORA = 512
A_KVLORA = 256
ROPE_THETA = 10000.0
B_HEADS = 8
B_HDIM = 128
B_PREV = 8
REL_CLIP = 128
SG_GROUPS = 8
SG_LEN = 128
B_WIDTH = B_HEADS * B_HDIM
A_WIDTH = A_HEADS * A_VDIM

VMEM_LIMIT_BYTES = 56 * 1024 * 1024

SUBLANES = 8

LATENT_W = 1024
Z_W = 6144
A_QBLK = 256
ATT_TQ = 256
BAND_KEYS = ATT_TQ + B_PREV * CHUNK

BF16 = jnp.bfloat16
F32 = jnp.float32
LOG2E = 1.4426950408889634


def _params(*sem):
    return pltpu.CompilerParams(dimension_semantics=sem, vmem_limit_bytes=VMEM_LIMIT_BYTES)


def _dot(a, b):
    return jnp.dot(a, b, preferred_element_type=F32)


def _dot_nt(a, b):
    return lax.dot_general(a, b, (((1,), (1,)), ((), ())), preferred_element_type=F32)


def _rms(x):
    return x * lax.rsqrt(jnp.mean(x * x, axis=-1, keepdims=True) + EPS)


def _silu(x):
    return x * jax.nn.sigmoid(x)


def _mod_kernel(c_ref, w_ref, b_ref, o_ref):
    cs = _silu(c_ref[...]).astype(BF16)
    o_ref[...] = _dot(cs, w_ref[...].astype(BF16)) + b_ref[...]


def _modulation(c, w_mod, b_mod, tn=512):
    depth, d, n = w_mod.shape
    bsz = c.shape[0]
    return pl.pallas_call(
        _mod_kernel,
        grid=(depth, n // tn),
        in_specs=[
            pl.BlockSpec((bsz, d), lambda l, j: (0, 0)),
            pl.BlockSpec((None, d, tn), lambda l, j: (l, 0, j)),
            pl.BlockSpec((None, 1, tn), lambda l, j: (l, 0, j)),
        ],
        out_specs=pl.BlockSpec((None, bsz, tn), lambda l, j: (l, 0, j)),
        out_shape=jax.ShapeDtypeStruct((depth, bsz, n), F32),
        compiler_params=_params("arbitrary", "arbitrary"),
        name="modulation",
    )(c, w_mod, b_mod.reshape(depth, 1, n))


def _adaln(x, g_ref, scale_ref, shift_ref):
    return (_rms(x) * g_ref[...] * (1.0 + scale_ref[...]) + shift_ref[...]).astype(BF16)


def _prenorm_kernel(x_ref, g_ref, scale_ref, shift_ref, h_ref):
    h_ref[...] = _adaln(x_ref[...], g_ref, scale_ref, shift_ref)


def _prenorm(x2, g_pre, scale, shift, seq, tm=1024):
    t, d = x2.shape
    per_b = seq // tm
    vec = pl.BlockSpec((None, 1, d), lambda m: (m // per_b, 0, 0))
    return pl.pallas_call(
        _prenorm_kernel,
        grid=(t // tm,),
        in_specs=[pl.BlockSpec((tm, d), lambda m: (m, 0)),
                  pl.BlockSpec((1, d), lambda m: (0, 0)), vec, vec],
        out_specs=pl.BlockSpec((tm, d), lambda m: (m, 0)),
        out_shape=jax.ShapeDtypeStruct((t, d), BF16),
        compiler_params=_params("arbitrary"),
        name="prenorm",
    )(x2, g_pre, scale, shift)


def _inproj_kernel(transposed, h_ref, w_ref, o_ref, wb_ref):
    @pl.when(pl.program_id(1) == 0)
    def _():
        wb_ref[...] = (w_ref[0].T if transposed else w_ref[...]).astype(BF16)

    o_ref[...] = _dot(h_ref[...], wb_ref[...]).astype(BF16)


def _inproj(h, w_stack, layer, n_out, transposed=False, carry=0, tm=2048, tn=1024):
    t, d = h.shape
    if transposed:
        w_spec = pl.BlockSpec(
            (pl.Element(1), pl.Element(tn), pl.Element(d)),
            lambda n, m: (layer, pl.multiple_of(jnp.maximum(n * tn - carry, 0), SUBLANES), 0))
    else:
        assert carry == 0
        w_spec = pl.BlockSpec((None, d, tn), lambda n, m: (layer, 0, n))
    return pl.pallas_call(
        functools.partial(_inproj_kernel, transposed),
        grid=(n_out // tn, t // tm),
        in_specs=[pl.BlockSpec((tm, d), lambda n, m: (m, 0)), w_spec],
        out_specs=pl.BlockSpec((tm, tn), lambda n, m: (m, n)),
        out_shape=jax.ShapeDtypeStruct((t, n_out), BF16),
        scratch_shapes=[pltpu.VMEM((d, tn), BF16)],
        compiler_params=_params("arbitrary", "arbitrary"),
        name="inproj",
    )(h, w_stack)


def _latent_kernel(z_ref, gq_ref, wq_ref, gkv_ref, wkv_ref, cs_ref,
                   q_ref, kn_ref, v_ref, kr_ref):
    z = z_ref[...].astype(F32)
    cs = cs_ref[...]
    cq = (_rms(z[:, :A_QLORA]) * gq_ref[...]).astype(BF16)
    q = _dot(cq, wq_ref[...])
    qscale = LOG2E * float(A_NOPE + A_ROPE) ** -0.5
    for h in range(A_HEADS):
        lo = h * A_QBLK
        q_ref[:, lo:lo + A_NOPE] = (q[:, lo:lo + A_NOPE] * qscale).astype(BF16)
        pr = q[:, lo + A_NOPE:lo + A_QBLK] * cs
        pr = pr + pltpu.roll(pr, A_ROPE, 1)
        q_ref[:, lo + A_NOPE:lo + A_QBLK] = (pr * qscale).astype(BF16)
    ckv = (_rms(z[:, A_QLORA:A_QLORA + A_KVLORA]) * gkv_ref[...]).astype(BF16)
    kv = _dot(ckv, wkv_ref[...])
    kn_ref[...] = kv[:, :A_WIDTH].astype(BF16)
    v_ref[...] = kv[:, A_WIDTH:].astype(BF16)
    o = A_QLORA + A_KVLORA
    kr = z[:, o:o + 2 * A_ROPE]
    lane = lax.broadcasted_iota(jnp.int32, kr.shape, 1)
    half = A_ROPE // 2
    swapped = jnp.where(lane < half, -pltpu.roll(kr, 2 * A_ROPE - half, 1), pltpu.roll(kr, half, 1))
    roped = kr * cs + swapped * pltpu.roll(cs, A_ROPE, 1)
    kr_ref[...] = jnp.where(lane < A_ROPE, roped, 0.0).astype(BF16)


def _latent(z, g_q, w_uq, g_kv, w_ukv, cs_tab, seq, tm=1024):
    t = z.shape[0]
    per_b = seq // tm
    const = lambda m: (0, 0)
    return pl.pallas_call(
        _latent_kernel,
        grid=(t // tm,),
        in_specs=[
            pl.BlockSpec((tm, LATENT_W), lambda m: (m, 0)),
            pl.BlockSpec((1, A_QLORA), const),
            pl.BlockSpec(w_uq.shape, const),
            pl.BlockSpec((1, A_KVLORA), const),
            pl.BlockSpec(w_ukv.shape, const),
            pl.BlockSpec((tm, 2 * A_ROPE), lambda m: (m % per_b, 0)),
        ],
        out_specs=[
            pl.BlockSpec((tm, A_HEADS * A_QBLK), lambda m: (m, 0)),
            pl.BlockSpec((tm, A_WIDTH), lambda m: (m, 0)),
            pl.BlockSpec((tm, A_WIDTH), lambda m: (m, 0)),
            pl.BlockSpec((tm, 2 * A_ROPE), lambda m: (m, 0)),
        ],
        out_shape=[
            jax.ShapeDtypeStruct((t, A_HEADS * A_QBLK), BF16),
            jax.ShapeDtypeStruct((t, A_WIDTH), BF16),
            jax.ShapeDtypeStruct((t, A_WIDTH), BF16),
            jax.ShapeDtypeStruct((t, 2 * A_ROPE), BF16),
        ],
        compiler_params=_params("arbitrary"),
        name="latent",
    )(z, g_q, w_uq, g_kv, w_ukv, cs_tab)


def _with_ones_column(v):
    lane = lax.broadcasted_iota(jnp.int32, v.shape, 1)
    return jnp.concatenate([v, jnp.where(lane == 0, 1.0, 0.0).astype(v.dtype)], axis=1)


def _softmax_pv(s_parts, v_parts):
    m = functools.reduce(jnp.maximum, [jnp.max(s, axis=-1, keepdims=True) for s in s_parts])
    acc = functools.reduce(
        jnp.add, [_dot(jnp.exp2(s - m).astype(BF16), v) for s, v in zip(s_parts, v_parts)])
    width = acc.shape[1] // 2
    return acc[:, :width] / acc[:, width:width + 1]


def _chunk_causal_mask(n):
    r = lax.broadcasted_iota(jnp.int32, (n, n), 0) // CHUNK
    c = lax.broadcasted_iota(jnp.int32, (n, n), 1) // CHUNK
    return c <= r


def _mla_attn_kernel(q_ref, kn_ref, kr_ref, v_ref, o_ref):
    seq = q_ref.shape[0]
    k = jnp.concatenate([kn_ref[...], kr_ref[...]], axis=1)
    v = _with_ones_column(v_ref[...])
    mask = _chunk_causal_mask(ATT_TQ)
    for i in reversed(range(seq // ATT_TQ)):
        lo = i * ATT_TQ
        q = q_ref[lo:lo + ATT_TQ, :]
        s_diag = jnp.where(mask, _dot_nt(q, k[lo:lo + ATT_TQ]), -jnp.inf)
        s_parts, v_parts = [s_diag], [v[lo:lo + ATT_TQ]]
        if i > 0:
            s_parts.append(_dot_nt(q, k[:lo]))
            v_parts.append(v[:lo])
        o_ref[lo:lo + ATT_TQ, :] = _softmax_pv(s_parts, v_parts).astype(BF16)


def _mla_attention(q, kn, kr, v):
    bsz, seq, _ = q.shape
    head = lambda b, h: (b, 0, h)
    return pl.pallas_call(
        _mla_attn_kernel,
        grid=(bsz, A_HEADS),
        in_specs=[
            pl.BlockSpec((None, seq, A_QBLK), head),
            pl.BlockSpec((None, seq, A_NOPE), head),
            pl.BlockSpec((None, seq, 2 * A_ROPE), lambda b, h: (b, 0, 0)),
            pl.BlockSpec((None, seq, A_VDIM), head),
        ],
        out_specs=pl.BlockSpec((None, seq, A_VDIM), head),
        out_shape=jax.ShapeDtypeStruct((bsz, seq, A_WIDTH), BF16),
        compiler_params=_params("arbitrary", "arbitrary"),
        name="mla_attention",
    )(q, kn, kr, v)


def _band_attn_kernel(q_ref, k_ref, v_ref, e_ref, o_ref, bias_ref):
    seq = q_ref.shape[0]
    scale = LOG2E * float(B_HDIM) ** -0.5

    @pl.when(pl.program_id(1) == 0)
    def _():
        e = jnp.broadcast_to(e_ref[...] * LOG2E, (ATT_TQ, e_ref.shape[1]))
        toep = pltpu.roll(e, 0, 1, stride=1, stride_axis=0)[:, :BAND_KEYS]
        qc = lax.broadcasted_iota(jnp.int32, toep.shape, 0) // CHUNK
        kc = lax.broadcasted_iota(jnp.int32, toep.shape, 1) // CHUNK - B_PREV
        bias_ref[...] = jnp.where((kc <= qc) & (kc >= qc - B_PREV), toep, -jnp.inf)

    v = _with_ones_column(v_ref[...])
    for i in reversed(range(seq // ATT_TQ)):
        lo = i * ATT_TQ
        klo = max(0, lo - B_PREV * CHUNK)
        nk = lo + ATT_TQ - klo
        q = (q_ref[lo:lo + ATT_TQ, :].astype(F32) * scale).astype(BF16)
        s = _dot_nt(q, k_ref[klo:klo + nk, :]) + bias_ref[:, BAND_KEYS - nk:]
        o_ref[lo:lo + ATT_TQ, :] = _softmax_pv([s], [v[klo:klo + nk]]).astype(BF16)


def _band_attention(z3, e_tab):
    bsz, seq, _ = z3.shape
    nb = LATENT_W // B_HDIM

    def col(off):
        return pl.BlockSpec((None, seq, B_HDIM), lambda h, b: (b, 0, off + h))

    return pl.pallas_call(
        _band_attn_kernel,
        grid=(B_HEADS, bsz),
        in_specs=[
            col(nb), col(nb + B_HEADS), col(nb + 2 * B_HEADS),
            pl.BlockSpec((None, 1, e_tab.shape[2]), lambda h, b: (h, 0, 0)),
        ],
        out_specs=pl.BlockSpec((None, seq, B_HDIM), lambda h, b: (b, 0, h)),
        out_shape=jax.ShapeDtypeStruct((bsz, seq, B_WIDTH), BF16),
        scratch_shapes=[pltpu.VMEM((ATT_TQ, BAND_KEYS), F32)],
        compiler_params=_params("arbitrary", "arbitrary"),
        name="band_attention",
    )(z3, z3, z3, e_tab)


def _band_bias_row(rel_table):
    h = rel_table.shape[0]
    width = 2 * (B_PREV * CHUNK)
    n_hi = B_PREV * CHUNK - REL_CLIP
    n_lo = width - n_hi - (2 * REL_CLIP + 1) - (CHUNK - 1)
    hi = rel_table[:, 2 * REL_CLIP:]
    lo = rel_table[:, :1]
    e = jnp.concatenate([jnp.broadcast_to(hi, (h, n_hi)), rel_table[:, ::-1],
                         jnp.broadcast_to(lo, (h, n_lo)),
                         jnp.broadcast_to(hi, (h, CHUNK - 1))], axis=1)
    return e.astype(F32)[:, None, :]


PIPE_LAG = 1


def _finish(y_fn, refs, emit_h):
    x_ref, w_ref, gpost_ref, mgate_ref = refs[:4]
    o_ref = refs[7 if emit_h else 4]
    wb_ref, r0_ref, r1_ref = refs[-3:]
    i = pl.program_id(0)

    @pl.when(i == 0)
    def _():
        wb_ref[...] = w_ref[...].astype(BF16)
        r1_ref[...] = jnp.zeros_like(r1_ref)

    def step(r_prev, r_cur):
        xn = x_ref[...] + mgate_ref[...] * (_rms(r_prev[...]) * gpost_ref[...])
        o_ref[...] = xn
        if emit_h:
            refs[8][...] = _adaln(xn, *refs[4:7])
        r_cur[...] = _dot(y_fn().astype(BF16), wb_ref[...])

    pl.when(i % 2 == 0)(lambda: step(r1_ref, r0_ref))
    pl.when(i % 2 == 1)(lambda: step(r0_ref, r1_ref))


def _finish_specs(x2, w_stack, layer, g_post, mgate, nxt, seq, tm):
    t, d = x2.shape
    per_b = seq // tm
    prev = lambda i: jnp.maximum(i - PIPE_LAG, 0)
    row = pl.BlockSpec((tm, d), lambda i: (prev(i), 0))
    vec = pl.BlockSpec((1, d), lambda i: (0, 0))
    bvec = pl.BlockSpec((None, 1, d), lambda i: (prev(i) // per_b, 0, 0))
    whole = pl.BlockSpec((None, d, d), lambda i: (layer, 0, 0), pipeline_mode=pl.Buffered(1))
    args, in_specs = [x2, w_stack, g_post, mgate], [row, whole, vec, bvec]
    out_specs, out_shape = [row], [jax.ShapeDtypeStruct((t, d), F32)]
    if nxt is not None:
        args += list(nxt)
        in_specs += [vec, bvec, bvec]
        out_specs.append(row)
        out_shape.append(jax.ShapeDtypeStruct((t, d), BF16))
    scratch = [pltpu.VMEM((d, d), BF16), pltpu.VMEM((tm, d), F32), pltpu.VMEM((tm, d), F32)]
    return args, in_specs, out_specs, out_shape, scratch


def _outproj_even_kernel(emit_h, oa_ref, ob_ref, gate_ref, *refs):
    def y_fn():
        o = jnp.concatenate([oa_ref[...], ob_ref[...]], axis=1).astype(F32)
        return o * _silu(gate_ref[...].astype(F32))

    _finish(y_fn, refs, emit_h)


def _outproj_even(oa, ob, z, x2, w_stack, layer, g_post, mgate, nxt, seq, tm=256):
    t, d = x2.shape
    n_tiles = t // tm
    cur = lambda i: jnp.minimum(i, n_tiles - 1)
    args, in_specs, out_specs, out_shape, scratch = _finish_specs(
        x2, w_stack, layer, g_post, mgate, nxt, seq, tm)
    return pl.pallas_call(
        functools.partial(_outproj_even_kernel, nxt is not None),
        grid=(n_tiles + PIPE_LAG,),
        in_specs=[
            pl.BlockSpec((tm, A_WIDTH), lambda i: (cur(i), 0)),
            pl.BlockSpec((tm, B_WIDTH), lambda i: (cur(i), 0)),
            pl.BlockSpec((tm, d), lambda i: (cur(i), 2)),
        ] + in_specs,
        out_specs=out_specs,
        out_shape=out_shape,
        scratch_shapes=scratch,
        compiler_params=_params("arbitrary"),
        name="outproj_even",
    )(oa, ob, z, *args)


def _sgu_kernel(emit_h, u_ref, v_ref, gate_ref, lng_ref, lnb_ref, ws_ref, bs_ref, *refs):
    sv_ref, refs = refs[-1], refs[:-1]
    tm, d = sv_ref.shape
    dg = d // SG_GROUPS

    def y_fn():
        v = v_ref[...].astype(F32)
        mu = jnp.mean(v, axis=-1, keepdims=True)
        vc = v - mu
        var = jnp.mean(vc * vc, axis=-1, keepdims=True)
        vn = ((vc * lax.rsqrt(var + EPS)) * lng_ref[...] + lnb_ref[...]).astype(BF16)
        cpos_r = lax.broadcasted_iota(jnp.int32, (SG_LEN, SG_LEN), 0) // CHUNK
        cpos_c = lax.broadcasted_iota(jnp.int32, (SG_LEN, SG_LEN), 1) // CHUNK
        mask = cpos_c <= cpos_r
        for g in range(SG_GROUPS):
            ws = jnp.where(mask, ws_ref[g], 0.0).astype(BF16)
            bs = bs_ref[:, g:g + 1]
            for n in range(tm // SG_LEN):
                blk = vn[n * SG_LEN:(n + 1) * SG_LEN, g * dg:(g + 1) * dg]
                sv_ref[n * SG_LEN:(n + 1) * SG_LEN, g * dg:(g + 1) * dg] = _dot(ws, blk) + bs
        return u_ref[...].astype(F32) * sv_ref[...] * _silu(gate_ref[...].astype(F32))

    _finish(y_fn, refs, emit_h)


def _sgu_outproj(z, x2, ln_g, ln_b, w_s, b_s_t, w_stack, layer, g_post, mgate, nxt, seq, tm=256):
    t, d = x2.shape
    n_tiles = t // tm
    cur = lambda i: jnp.minimum(i, n_tiles - 1)
    const2 = lambda i: (0, 0)
    args, in_specs, out_specs, out_shape, scratch = _finish_specs(
        x2, w_stack, layer, g_post, mgate, nxt, seq, tm)
    return pl.pallas_call(
        functools.partial(_sgu_kernel, nxt is not None),
        grid=(n_tiles + PIPE_LAG,),
        in_specs=[
            pl.BlockSpec((tm, d), lambda i: (cur(i), 0)),
            pl.BlockSpec((tm, d), lambda i: (cur(i), 1)),
            pl.BlockSpec((tm, d), lambda i: (cur(i), 2)),
            pl.BlockSpec((1, d), const2),
            pl.BlockSpec((1, d), const2),
            pl.BlockSpec(w_s.shape, lambda i: (0, 0, 0)),
            pl.BlockSpec(b_s_t.shape, const2),
        ] + in_specs,
        out_specs=out_specs,
        out_shape=out_shape,
        scratch_shapes=scratch + [pltpu.VMEM((tm, d), F32)],
        compiler_params=_params("arbitrary"),
        name="sgu_outproj",
    )(z, z, z, ln_g, ln_b, w_s, b_s_t, *args)


def _half_swap(w):
    half = w.shape[-1] // 2
    return jnp.concatenate([-w[..., half:], w[..., :half]], axis=-1)


def _prep_latent_weights(w_uq, w_ukv):
    wq = w_uq.reshape(A_QLORA, A_HEADS, A_NOPE + A_ROPE)
    wq_rope = wq[..., A_NOPE:]
    wq_p = jnp.concatenate([wq, _half_swap(wq_rope)], axis=-1).reshape(A_QLORA, A_HEADS * A_QBLK)
    wkv = w_ukv.reshape(A_KVLORA, A_HEADS, A_NOPE + A_VDIM)
    wkv_p = jnp.concatenate([wkv[..., :A_NOPE].reshape(A_KVLORA, A_WIDTH),
                             wkv[..., A_NOPE:].reshape(A_KVLORA, A_WIDTH)], axis=1)
    return wq_p.astype(BF16), wkv_p.astype(BF16)


def _rope_table(seq):
    half = A_ROPE // 2
    pos = jnp.arange(seq, dtype=F32)
    freqs = ROPE_THETA ** (-jnp.arange(half, dtype=F32) / half)
    ang = pos[:, None] * freqs[None, :]
    cos, sin = jnp.cos(ang), jnp.sin(ang)
    return jnp.concatenate([cos, cos, sin, sin], axis=1)


def kernel(x, c, w_mod, b_mod, g_pre, g_post, ab_w_in, a_g_q, a_w_uq, a_g_kv, a_w_ukv,
           b_rel_bias, ab_w_out, sg_w_in, sg_ln_g, sg_ln_b, sg_w_s, sg_b_s, sg_w_out):
    bsz, seq, d = x.shape
    t = bsz * seq
    mod = _modulation(c, w_mod, b_mod)
    cs_tab = _rope_table(seq)
    x2 = x.reshape(t, d)

    def adaln_args(l):
        return g_pre[l][None], mod[l, :, None, d:2 * d], mod[l, :, None, :d]

    ab_w_in_t = jnp.swapaxes(ab_w_in, 1, 2)
    h = _prenorm(x2, *adaln_args(0), seq)
    for l in range(DEPTH):
        mgate = mod[l, :, None, 2 * d:]
        nxt = adaln_args(l + 1) if l + 1 < DEPTH else None
        i = l // 2
        if l % 2 == 0:
            w_uq, w_ukv = _prep_latent_weights(a_w_uq[i], a_w_ukv[i])
            z = _inproj(h, ab_w_in_t, i, Z_W, transposed=True,
                        carry=LATENT_W - (A_QLORA + A_KVLORA + A_ROPE))
            q, kn, v, kr = _latent(z, a_g_q[i][None], w_uq, a_g_kv[i][None], w_ukv, cs_tab, seq)
            oa = _mla_attention(q.reshape(bsz, seq, -1), kn.reshape(bsz, seq, -1),
                                kr.reshape(bsz, seq, -1), v.reshape(bsz, seq, -1))
            ob = _band_attention(z.reshape(bsz, seq, Z_W), _band_bias_row(b_rel_bias[i]))
            out = _outproj_even(oa.reshape(t, A_WIDTH), ob.reshape(t, B_WIDTH), z, x2,
                                ab_w_out, i, g_post[l][None], mgate, nxt, seq)
        else:
            z = _inproj(h, sg_w_in, i, sg_w_in.shape[2])
            out = _sgu_outproj(z, x2, sg_ln_g[i][None], sg_ln_b[i][None], sg_w_s[i],
                               sg_b_s[i].T, sg_w_out, i, g_post[l][None], mgate, nxt, seq)
        x2, h = out if nxt is not None else (out[0], None)
    return x2.reshape(bsz, seq, d)
```

```python
import functools

import jax
import jax.numpy as jnp
from jax import lax
from jax.experimental import pallas as pl
from jax.experimental.pallas import tpu as pltpu

D_MODEL = 2048
DEPTH = 4
CHUNK = 64
EPS = 1e-6
A_HEADS = 8
A_NOPE = 128
A_ROPE = 64
A_VDIM = 128
A_QLORA = 512
A_KVLORA = 256
ROPE_THETA = 10000.0
B_HEADS = 8
B_HDIM = 128
B_PREV = 8
REL_CLIP = 128
SG_GROUPS = 8
SG_LEN = 128
B_WIDTH = B_HEADS * B_HDIM
A_WIDTH = A_HEADS * A_VDIM

VMEM_LIMIT_BYTES = 56 * 1024 * 1024

SUBLANES = 8

LATENT_W = 1024
Z_W = 6144
A_QBLK = 256
ATT_TQ = 256
BAND_KEYS = ATT_TQ + B_PREV * CHUNK

BF16 = jnp.bfloat16
F32 = jnp.float32
LOG2E = 1.4426950408889634


def _params(*sem):
    return pltpu.CompilerParams(dimension_semantics=sem, vmem_limit_bytes=VMEM_LIMIT_BYTES)


def _dot(a, b):
    return jnp.dot(a, b, preferred_element_type=F32)


def _dot_nt(a, b):
    return lax.dot_general(a, b, (((1,), (1,)), ((), ())), preferred_element_type=F32)


def _rms(x):
    return x * lax.rsqrt(jnp.mean(x * x, axis=-1, keepdims=True) + EPS)


def _silu(x):
    return x * jax.nn.sigmoid(x)


def _mod_kernel(c_ref, w_ref, b_ref, o_ref):
    cs = _silu(c_ref[...]).astype(BF16)
    o_ref[...] = _dot(cs, w_ref[...].astype(BF16)) + b_ref[...]


def _modulation(c, w_mod, b_mod, tn=512):
    depth, d, n = w_mod.shape
    bsz = c.shape[0]
    return pl.pallas_call(
        _mod_kernel,
        grid=(depth, n // tn),
        in_specs=[
            pl.BlockSpec((bsz, d), lambda l, j: (0, 0)),
            pl.BlockSpec((None, d, tn), lambda l, j: (l, 0, j)),
            pl.BlockSpec((None, 1, tn), lambda l, j: (l, 0, j)),
        ],
        out_specs=pl.BlockSpec((None, bsz, tn), lambda l, j: (l, 0, j)),
        out_shape=jax.ShapeDtypeStruct((depth, bsz, n), F32),
        compiler_params=_params("arbitrary", "arbitrary"),
        name="modulation",
    )(c, w_mod, b_mod.reshape(depth, 1, n))


def _adaln(x, g_ref, scale_ref, shift_ref):
    gain = g_ref[...] * (1.0 + scale_ref[...])
    return (_rms(x) * gain + shift_ref[...]).astype(BF16)


def _prenorm_kernel(x_ref, g_ref, scale_ref, shift_ref, h_ref):
    h_ref[...] = _adaln(x_ref[...], g_ref, scale_ref, shift_ref)


def _prenorm(x2, g_pre, scale, shift, seq, tm=1024):
    t, d = x2.shape
    per_b = seq // tm
    vec = pl.BlockSpec((None, 1, d), lambda m: (m // per_b, 0, 0))
    return pl.pallas_call(
        _prenorm_kernel,
        grid=(t // tm,),
        in_specs=[pl.BlockSpec((tm, d), lambda m: (m, 0)),
                  pl.BlockSpec((1, d), lambda m: (0, 0)), vec, vec],
        out_specs=pl.BlockSpec((tm, d), lambda m: (m, 0)),
        out_shape=jax.ShapeDtypeStruct((t, d), BF16),
        compiler_params=_params("arbitrary"),
        name="prenorm",
    )(x2, g_pre, scale, shift)


def _inproj_kernel(transposed, h_ref, w_ref, o_ref, wb_ref):
    @pl.when(pl.program_id(1) == 0)
    def _():
        wb_ref[...] = (w_ref[0].T if transposed else w_ref[...]).astype(BF16)

    o_ref[...] = _dot(h_ref[...], wb_ref[...]).astype(BF16)


def _inproj(h, w_stack, layer, n_out, transposed=False, carry=0, tm=2048, tn=1024):
    t, d = h.shape
    if transposed:
        w_spec = pl.BlockSpec(
            (pl.Element(1), pl.Element(tn), pl.Element(d)),
            lambda n, m: (layer, pl.multiple_of(jnp.maximum(n * tn - carry, 0), SUBLANES), 0))
    else:
        assert carry == 0
        w_spec = pl.BlockSpec((None, d, tn), lambda n, m: (layer, 0, n))
    return pl.pallas_call(
        functools.partial(_inproj_kernel, transposed),
        grid=(n_out // tn, t // tm),
        in_specs=[pl.BlockSpec((tm, d), lambda n, m: (m, 0)), w_spec],
        out_specs=pl.BlockSpec((tm, tn), lambda n, m: (m, n)),
        out_shape=jax.ShapeDtypeStruct((t, n_out), BF16),
        scratch_shapes=[pltpu.VMEM((d, tn), BF16)],
        compiler_params=_params("arbitrary", "arbitrary"),
        name="inproj",
    )(h, w_stack)


def _latent_kernel(z_ref, gq_ref, wq_ref, gkv_ref, wkv_ref, cs_ref,
                   q_ref, kn_ref, v_ref, kr_ref):
    z = z_ref[...].astype(F32)
    cs = cs_ref[...]
    cq = (_rms(z[:, :A_QLORA]) * gq_ref[...]).astype(BF16)
    q = _dot(cq, wq_ref[...])
    qscale = LOG2E * float(A_NOPE + A_ROPE) ** -0.5
    for h in range(A_HEADS):
        lo = h * A_QBLK
        q_ref[:, lo:lo + A_NOPE] = (q[:, lo:lo + A_NOPE] * qscale).astype(BF16)
        pr = q[:, lo + A_NOPE:lo + A_QBLK] * cs
        pr = pr + pltpu.roll(pr, A_ROPE, 1)
        q_ref[:, lo + A_NOPE:lo + A_QBLK] = (pr * qscale).astype(BF16)
    ckv = (_rms(z[:, A_QLORA:A_QLORA + A_KVLORA]) * gkv_ref[...]).astype(BF16)
    kv = _dot(ckv, wkv_ref[...])
    kn_ref[...] = kv[:, :A_WIDTH].astype(BF16)
    v_ref[...] = kv[:, A_WIDTH:].astype(BF16)
    o = A_QLORA + A_KVLORA
    kr = z[:, o:o + 2 * A_ROPE]
    lane = lax.broadcasted_iota(jnp.int32, kr.shape, 1)
    half = A_ROPE // 2
    swapped = jnp.where(lane < half, -pltpu.roll(kr, 2 * A_ROPE - half, 1), pltpu.roll(kr, half, 1))
    roped = kr * cs + swapped * pltpu.roll(cs, A_ROPE, 1)
    kr_ref[...] = jnp.where(lane < A_ROPE, roped, 0.0).astype(BF16)


def _latent(z, g_q, w_uq, g_kv, w_ukv, cs_tab, seq, tm=1024):
    t = z.shape[0]
    per_b = seq // tm
    const = lambda m: (0, 0)
    return pl.pallas_call(
        _latent_kernel,
        grid=(t // tm,),
        in_specs=[
            pl.BlockSpec((tm, LATENT_W), lambda m: (m, 0)),
            pl.BlockSpec((1, A_QLORA), const),
            pl.BlockSpec(w_uq.shape, const),
            pl.BlockSpec((1, A_KVLORA), const),
            pl.BlockSpec(w_ukv.shape, const),
            pl.BlockSpec((tm, 2 * A_ROPE), lambda m: (m % per_b, 0)),
        ],
        out_specs=[
            pl.BlockSpec((tm, A_HEADS * A_QBLK), lambda m: (m, 0)),
            pl.BlockSpec((tm, A_WIDTH), lambda m: (m, 0)),
            pl.BlockSpec((tm, A_WIDTH), lambda m: (m, 0)),
            pl.BlockSpec((tm, 2 * A_ROPE), lambda m: (m, 0)),
        ],
        out_shape=[
            jax.ShapeDtypeStruct((t, A_HEADS * A_QBLK), BF16),
            jax.ShapeDtypeStruct((t, A_WIDTH), BF16),
            jax.ShapeDtypeStruct((t, A_WIDTH), BF16),
            jax.ShapeDtypeStruct((t, 2 * A_ROPE), BF16),
        ],
        compiler_params=_params("arbitrary"),
        name="latent",
    )(z, g_q, w_uq, g_kv, w_ukv, cs_tab)


def _with_ones_column(v):
    lane = lax.broadcasted_iota(jnp.int32, v.shape, 1)
    return jnp.concatenate([v, jnp.where(lane == 0, 1.0, 0.0).astype(v.dtype)], axis=1)


def _softmax_pv(s_parts, v_parts):
    m = functools.reduce(jnp.maximum, [jnp.max(s, axis=-1, keepdims=True) for s in s_parts])
    acc = functools.reduce(
        jnp.add, [_dot(jnp.exp2(s - m).astype(BF16), v) for s, v in zip(s_parts, v_parts)])
    width = acc.shape[1] // 2
    return acc[:, :width] / acc[:, width:width + 1]


def _chunk_causal_mask(n):
    r = lax.broadcasted_iota(jnp.int32, (n, n), 0) // CHUNK
    c = lax.broadcasted_iota(jnp.int32, (n, n), 1) // CHUNK
    return c <= r


def _mla_attn_kernel(q_ref, kn_ref, kr_ref, v_ref, o_ref):
    seq = q_ref.shape[0]
    k = jnp.concatenate([kn_ref[...], kr_ref[...]], axis=1)
    v = _with_ones_column(v_ref[...])
    mask = _chunk_causal_mask(ATT_TQ)
    for i in reversed(range(seq // ATT_TQ)):
        lo = i * ATT_TQ
        q = q_ref[lo:lo + ATT_TQ, :]
        s_diag = jnp.where(mask, _dot_nt(q, k[lo:lo + ATT_TQ]), -jnp.inf)
        s_parts, v_parts = [s_diag], [v[lo:lo + ATT_TQ]]
        if i > 0:
            s_parts.append(_dot_nt(q, k[:lo]))
            v_parts.append(v[:lo])
        o_ref[lo:lo + ATT_TQ, :] = _softmax_pv(s_parts, v_parts).astype(BF16)


def _mla_attention(q, kn, kr, v):
    bsz, seq, _ = q.shape
    head = lambda b, h: (b, 0, h)
    return pl.pallas_call(
        _mla_attn_kernel,
        grid=(bsz, A_HEADS),
        in_specs=[
            pl.BlockSpec((None, seq, A_QBLK), head),
            pl.BlockSpec((None, seq, A_NOPE), head),
            pl.BlockSpec((None, seq, 2 * A_ROPE), lambda b, h: (b, 0, 0)),
            pl.BlockSpec((None, seq, A_VDIM), head),
        ],
        out_specs=pl.BlockSpec((None, seq, A_VDIM), head),
        out_shape=jax.ShapeDtypeStruct((bsz, seq, A_WIDTH), BF16),
        compiler_params=_params("arbitrary", "arbitrary"),
        name="mla_attention",
    )(q, kn, kr, v)


def _band_attn_kernel(q_ref, k_ref, v_ref, e_ref, o_ref, bias_ref):
    seq = q_ref.shape[0]
    scale = LOG2E * float(B_HDIM) ** -0.5

    @pl.when(pl.program_id(1) == 0)
    def _():
        e = jnp.broadcast_to(e_ref[...] * LOG2E, (ATT_TQ, e_ref.shape[1]))
        toep = pltpu.roll(e, 0, 1, stride=1, stride_axis=0)[:, :BAND_KEYS]
        qc = lax.broadcasted_iota(jnp.int32, toep.shape, 0) // CHUNK
        kc = lax.broadcasted_iota(jnp.int32, toep.shape, 1) // CHUNK - B_PREV
        bias_ref[...] = jnp.where((kc <= qc) & (kc >= qc - B_PREV), toep, -jnp.inf)

    v = _with_ones_column(v_ref[...])
    for i in reversed(range(seq // ATT_TQ)):
        lo = i * ATT_TQ
        klo = max(0, lo - B_PREV * CHUNK)
        nk = lo + ATT_TQ - klo
        q = (q_ref[lo:lo + ATT_TQ, :].astype(F32) * scale).astype(BF16)
        s = _dot_nt(q, k_ref[klo:klo + nk, :]) + bias_ref[:, BAND_KEYS - nk:]
        o_ref[lo:lo + ATT_TQ, :] = _softmax_pv([s], [v[klo:klo + nk]]).astype(BF16)


def _band_attention(z3, e_tab):
    bsz, seq, _ = z3.shape
    nb = LATENT_W // B_HDIM

    def col(off):
        return pl.BlockSpec((None, seq, B_HDIM), lambda h, b: (b, 0, off + h))

    return pl.pallas_call(
        _band_attn_kernel,
        grid=(B_HEADS, bsz),
        in_specs=[
            col(nb), col(nb + B_HEADS), col(nb + 2 * B_HEADS),
            pl.BlockSpec((None, 1, e_tab.shape[2]), lambda h, b: (h, 0, 0)),
        ],
        out_specs=pl.BlockSpec((None, seq, B_HDIM), lambda h, b: (b, 0, h)),
        out_shape=jax.ShapeDtypeStruct((bsz, seq, B_WIDTH), BF16),
        scratch_shapes=[pltpu.VMEM((ATT_TQ, BAND_KEYS), F32)],
        compiler_params=_params("arbitrary", "arbitrary"),
        name="band_attention",
    )(z3, z3, z3, e_tab)


def _band_bias_row(rel_table):
    h = rel_table.shape[0]
    width = 2 * (B_PREV * CHUNK)
    n_hi = B_PREV * CHUNK - REL_CLIP
    n_lo = width - n_hi - (2 * REL_CLIP + 1) - (CHUNK - 1)
    hi = rel_table[:, 2 * REL_CLIP:]
    lo = rel_table[:, :1]
    e = jnp.concatenate([jnp.broadcast_to(hi, (h, n_hi)), rel_table[:, ::-1],
                         jnp.broadcast_to(lo, (h, n_lo)),
                         jnp.broadcast_to(hi, (h, CHUNK - 1))], axis=1)
    return e.astype(F32)[:, None, :]


PIPE_LAG = 1


def _finish(y_fn, refs, emit_h):
    x_ref, w_ref, gpost_ref, mgate_ref = refs[:4]
    o_ref = refs[7 if emit_h else 4]
    wb_ref, r0_ref, r1_ref = refs[-3:]
    i = pl.program_id(0)

    @pl.when(i == 0)
    def _():
        wb_ref[...] = w_ref[...].astype(BF16)
        r1_ref[...] = jnp.zeros_like(r1_ref)

    def step(r_prev, r_cur):
        xn = x_ref[...] + _rms(r_prev[...]) * (mgate_ref[...] * gpost_ref[...])
        o_ref[...] = xn
        if emit_h:
            refs[8][...] = _adaln(xn, *refs[4:7])
        r_cur[...] = _dot(y_fn().astype(BF16), wb_ref[...])

    pl.when(i % 2 == 0)(lambda: step(r1_ref, r0_ref))
    pl.when(i % 2 == 1)(lambda: step(r0_ref, r1_ref))


def _finish_specs(x2, w_stack, layer, g_post, mgate, nxt, seq, tm):
    t, d = x2.shape
    per_b = seq // tm
    prev = lambda i: jnp.maximum(i - PIPE_LAG, 0)
    row = pl.BlockSpec((tm, d), lambda i: (prev(i), 0))
    vec = pl.BlockSpec((1, d), lambda i: (0, 0))
    bvec = pl.BlockSpec((None, 1, d), lambda i: (prev(i) // per_b, 0, 0))
    whole = pl.BlockSpec((None, d, d), lambda i: (layer, 0, 0), pipeline_mode=pl.Buffered(1))
    args, in_specs = [x2, w_stack, g_post, mgate], [row, whole, vec, bvec]
    out_specs, out_shape = [row], [jax.ShapeDtypeStruct((t, d), F32)]
    if nxt is not None:
        args += list(nxt)
        in_specs += [vec, bvec, bvec]
        out_specs.append(row)
        out_shape.append(jax.ShapeDtypeStruct((t, d), BF16))
    scratch = [pltpu.VMEM((d, d), BF16), pltpu.VMEM((tm, d), F32), pltpu.VMEM((tm, d), F32)]
    return args, in_specs, out_specs, out_shape, scratch


def _outproj_even_kernel(emit_h, oa_ref, ob_ref, gate_ref, *refs):
    def y_fn():
        o = jnp.concatenate([oa_ref[...], ob_ref[...]], axis=1).astype(F32)
        return o * _silu(gate_ref[...].astype(F32))

    _finish(y_fn, refs, emit_h)


def _outproj_even(oa, ob, z, x2, w_stack, layer, g_post, mgate, nxt, seq, tm=256):
    t, d = x2.shape
    n_tiles = t // tm
    cur = lambda i: jnp.minimum(i, n_tiles - 1)
    args, in_specs, out_specs, out_shape, scratch = _finish_specs(
        x2, w_stack, layer, g_post, mgate, nxt, seq, tm)
    return pl.pallas_call(
        functools.partial(_outproj_even_kernel, nxt is not None),
        grid=(n_tiles + PIPE_LAG,),
        in_specs=[
            pl.BlockSpec((tm, A_WIDTH), lambda i: (cur(i), 0)),
            pl.BlockSpec((tm, B_WIDTH), lambda i: (cur(i), 0)),
            pl.BlockSpec((tm, d), lambda i: (cur(i), 2)),
        ] + in_specs,
        out_specs=out_specs,
        out_shape=out_shape,
        scratch_shapes=scratch,
        compiler_params=_params("arbitrary"),
        name="outproj_even",
    )(oa, ob, z, *args)


def _sgu_kernel(emit_h, u_ref, v_ref, gate_ref, lng_ref, lnb_ref, ws_ref, bs_ref, *refs):
    sv_ref, refs = refs[-1], refs[:-1]
    tm, d = sv_ref.shape
    dg = d // SG_GROUPS

    def y_fn():
        v = v_ref[...].astype(F32)
        mu = jnp.mean(v, axis=-1, keepdims=True)
        vc = v - mu
        var = jnp.mean(vc * vc, axis=-1, keepdims=True)
        vn = ((vc * lax.rsqrt(var + EPS)) * lng_ref[...] + lnb_ref[...]).astype(BF16)
        cpos_r = lax.broadcasted_iota(jnp.int32, (SG_LEN, SG_LEN), 0) // CHUNK
        cpos_c = lax.broadcasted_iota(jnp.int32, (SG_LEN, SG_LEN), 1) // CHUNK
        mask = cpos_c <= cpos_r
        for g in range(SG_GROUPS):
            ws = jnp.where(mask, ws_ref[g], 0.0).astype(BF16)
            bs = bs_ref[:, g:g + 1]
            for n in range(tm // SG_LEN):
                blk = vn[n * SG_LEN:(n + 1) * SG_LEN, g * dg:(g + 1) * dg]
                sv_ref[n * SG_LEN:(n + 1) * SG_LEN, g * dg:(g + 1) * dg] = _dot(ws, blk) + bs
        return u_ref[...].astype(F32) * sv_ref[...] * _silu(gate_ref[...].astype(F32))

    _finish(y_fn, refs, emit_h)


def _sgu_outproj(z, x2, ln_g, ln_b, w_s, b_s_t, w_stack, layer, g_post, mgate, nxt, seq, tm=256):
    t, d = x2.shape
    n_tiles = t // tm
    cur = lambda i: jnp.minimum(i, n_tiles - 1)
    const2 = lambda i: (0, 0)
    args, in_specs, out_specs, out_shape, scratch = _finish_specs(
        x2, w_stack, layer, g_post, mgate, nxt, seq, tm)
    return pl.pallas_call(
        functools.partial(_sgu_kernel, nxt is not None),
        grid=(n_tiles + PIPE_LAG,),
        in_specs=[
            pl.BlockSpec((tm, d), lambda i: (cur(i), 0)),
            pl.BlockSpec((tm, d), lambda i: (cur(i), 1)),
            pl.BlockSpec((tm, d), lambda i: (cur(i), 2)),
            pl.BlockSpec((1, d), const2),
            pl.BlockSpec((1, d), const2),
            pl.BlockSpec(w_s.shape, lambda i: (0, 0, 0)),
            pl.BlockSpec(b_s_t.shape, const2),
        ] + in_specs,
        out_specs=out_specs,
        out_shape=out_shape,
        scratch_shapes=scratch + [pltpu.VMEM((tm, d), F32)],
        compiler_params=_params("arbitrary"),
        name="sgu_outproj",
    )(z, z, z, ln_g, ln_b, w_s, b_s_t, *args)


def _half_swap(w):
    half = w.shape[-1] // 2
    return jnp.concatenate([-w[..., half:], w[..., :half]], axis=-1)


def _prep_latent_weights(w_uq, w_ukv):
    wq = w_uq.reshape(A_QLORA, A_HEADS, A_NOPE + A_ROPE)
    wq_rope = wq[..., A_NOPE:]
    wq_p = jnp.concatenate([wq, _half_swap(wq_rope)], axis=-1).reshape(A_QLORA, A_HEADS * A_QBLK)
    wkv = w_ukv.reshape(A_KVLORA, A_HEADS, A_NOPE + A_VDIM)
    wkv_p = jnp.concatenate([wkv[..., :A_NOPE].reshape(A_KVLORA, A_WIDTH),
                             wkv[..., A_NOPE:].reshape(A_KVLORA, A_WIDTH)], axis=1)
    return wq_p.astype(BF16), wkv_p.astype(BF16)


def _rope_table(seq):
    half = A_ROPE // 2
    pos = jnp.arange(seq, dtype=F32)
    freqs = ROPE_THETA ** (-jnp.arange(half, dtype=F32) / half)
    ang = pos[:, None] * freqs[None, :]
    cos, sin = jnp.cos(ang), jnp.sin(ang)
    return jnp.concatenate([cos, cos, sin, sin], axis=1)


def kernel(x, c, w_mod, b_mod, g_pre, g_post, ab_w_in, a_g_q, a_w_uq, a_g_kv, a_w_ukv,
           b_rel_bias, ab_w_out, sg_w_in, sg_ln_g, sg_ln_b, sg_w_s, sg_b_s, sg_w_out):
    bsz, seq, d = x.shape
    t = bsz * seq
    mod = _modulation(c, w_mod, b_mod)
    cs_tab = _rope_table(seq)
    x2 = x.reshape(t, d)

    def adaln_args(l):
        return g_pre[l][None], mod[l, :, None, d:2 * d], mod[l, :, None, :d]

    ab_w_in_t = jnp.swapaxes(ab_w_in, 1, 2)
    h = _prenorm(x2, *adaln_args(0), seq)
    for l in range(DEPTH):
        mgate = mod[l, :, None, 2 * d:]
        nxt = adaln_args(l + 1) if l + 1 < DEPTH else None
        i = l // 2
        if l % 2 == 0:
            w_uq, w_ukv = _prep_latent_weights(a_w_uq[i], a_w_ukv[i])
            z = _inproj(h, ab_w_in_t, i, Z_W, transposed=True,
                        carry=LATENT_W - (A_QLORA + A_KVLORA + A_ROPE))
            q, kn, v, kr = _latent(z, a_g_q[i][None], w_uq, a_g_kv[i][None], w_ukv, cs_tab, seq)
            oa = _mla_attention(q.reshape(bsz, seq, -1), kn.reshape(bsz, seq, -1),
                                kr.reshape(bsz, seq, -1), v.reshape(bsz, seq, -1))
            ob = _band_attention(z.reshape(bsz, seq, Z_W), _band_bias_row(b_rel_bias[i]))
            out = _outproj_even(oa.reshape(t, A_WIDTH), ob.reshape(t, B_WIDTH), z, x2,
                                ab_w_out, i, g_post[l][None], mgate, nxt, seq)
        else:
            z = _inproj(h, sg_w_in, i, sg_w_in.shape[2])
            out = _sgu_outproj(z, x2, sg_ln_g[i][None], sg_ln_b[i][None], sg_w_s[i],
                               sg_b_s[i].T, sg_w_out, i, g_post[l][None], mgate, nxt, seq)
        x2, h = out if nxt is not None else (out[0], None)
    return x2.reshape(bsz, seq, d)
```

```python
import functools

import jax
import jax.numpy as jnp
from jax import lax
from jax.experimental import pallas as pl
from jax.experimental.pallas import tpu as pltpu

D_MODEL = 2048
DEPTH = 4
CHUNK = 64
EPS = 1e-6
A_HEADS = 8
A_NOPE = 128
A_ROPE = 64
A_VDIM = 128
A_QLORA = 512
A_KVLORA = 256
ROPE_THETA = 10000.0
B_HEADS = 8
B_HDIM = 128
B_PREV = 8
REL_CLIP = 128
SG_GROUPS = 8
SG_LEN = 128
B_WIDTH = B_HEADS * B_HDIM
A_WIDTH = A_HEADS * A_VDIM

VMEM_LIMIT_BYTES = 56 * 1024 * 1024

SUBLANES = 8

LATENT_W = 1024
Z_W = 6144
A_QBLK = 256
ATT_TQ = 256
HEADS_PER_STEP = 4
BAND_KEYS = ATT_TQ + B_PREV * CHUNK

BF16 = jnp.bfloat16
F32 = jnp.float32
LOG2E = 1.4426950408889634


def _params(*sem):
    return pltpu.CompilerParams(dimension_semantics=sem, vmem_limit_bytes=VMEM_LIMIT_BYTES)


def _dot(a, b):
    return jnp.dot(a, b, preferred_element_type=F32)


def _dot_nt(a, b):
    return lax.dot_general(a, b, (((1,), (1,)), ((), ())), preferred_element_type=F32)


def _rms(x):
    return x * lax.rsqrt(jnp.mean(x * x, axis=-1, keepdims=True) + EPS)


def _silu(x):
    return x * jax.nn.sigmoid(x)


def _mod_kernel(c_ref, w_ref, b_ref, o_ref):
    cs = _silu(c_ref[...]).astype(BF16)
    o_ref[...] = _dot(cs, w_ref[...].astype(BF16)) + b_ref[...]


def _modulation(c, w_mod, b_mod, tn=512):
    depth, d, n = w_mod.shape
    bsz = c.shape[0]
    return pl.pallas_call(
        _mod_kernel,
        grid=(depth, n // tn),
        in_specs=[
            pl.BlockSpec((bsz, d), lambda l, j: (0, 0)),
            pl.BlockSpec((None, d, tn), lambda l, j: (l, 0, j)),
            pl.BlockSpec((None, 1, tn), lambda l, j: (l, 0, j)),
        ],
        out_specs=pl.BlockSpec((None, bsz, tn), lambda l, j: (l, 0, j)),
        out_shape=jax.ShapeDtypeStruct((depth, bsz, n), F32),
        compiler_params=_params("arbitrary", "arbitrary"),
        name="modulation",
    )(c, w_mod, b_mod.reshape(depth, 1, n))


def _adaln(x, g_ref, scale_ref, shift_ref):
    gain = g_ref[...] * (1.0 + scale_ref[...])
    return (_rms(x) * gain + shift_ref[...]).astype(BF16)


def _prenorm_kernel(x_ref, g_ref, scale_ref, shift_ref, h_ref):
    h_ref[...] = _adaln(x_ref[...], g_ref, scale_ref, shift_ref)


def _prenorm(x2, g_pre, scale, shift, seq, tm=1024):
    t, d = x2.shape
    per_b = seq // tm
    vec = pl.BlockSpec((None, 1, d), lambda m: (m // per_b, 0, 0))
    return pl.pallas_call(
        _prenorm_kernel,
        grid=(t // tm,),
        in_specs=[pl.BlockSpec((tm, d), lambda m: (m, 0)),
                  pl.BlockSpec((1, d), lambda m: (0, 0)), vec, vec],
        out_specs=pl.BlockSpec((tm, d), lambda m: (m, 0)),
        out_shape=jax.ShapeDtypeStruct((t, d), BF16),
        compiler_params=_params("arbitrary"),
        name="prenorm",
    )(x2, g_pre, scale, shift)


def _inproj_kernel(transposed, h_ref, w_ref, o_ref, wb_ref):
    @pl.when(pl.program_id(1) == 0)
    def _():
        wb_ref[...] = (w_ref[0].T if transposed else w_ref[...]).astype(BF16)

    o_ref[...] = _dot(h_ref[...], wb_ref[...]).astype(BF16)


def _inproj(h, w_stack, layer, n_out, transposed=False, carry=0, tm=2048, tn=1024):
    t, d = h.shape
    if transposed:
        w_spec = pl.BlockSpec(
            (pl.Element(1), pl.Element(tn), pl.Element(d)),
            lambda n, m: (layer, pl.multiple_of(jnp.maximum(n * tn - carry, 0), SUBLANES), 0))
    else:
        assert carry == 0
        w_spec = pl.BlockSpec((None, d, tn), lambda n, m: (layer, 0, n))
    return pl.pallas_call(
        functools.partial(_inproj_kernel, transposed),
        grid=(n_out // tn, t // tm),
        in_specs=[pl.BlockSpec((tm, d), lambda n, m: (m, 0)), w_spec],
        out_specs=pl.BlockSpec((tm, tn), lambda n, m: (m, n)),
        out_shape=jax.ShapeDtypeStruct((t, n_out), BF16),
        scratch_shapes=[pltpu.VMEM((d, tn), BF16)],
        compiler_params=_params("arbitrary", "arbitrary"),
        name="inproj",
    )(h, w_stack)


def _latent_kernel(z_ref, gq_ref, wq_ref, gkv_ref, wkv_ref, cs_ref,
                   q_ref, kn_ref, v_ref, kr_ref):
    z = z_ref[...].astype(F32)
    cs = cs_ref[...]
    cq = (_rms(z[:, :A_QLORA]) * gq_ref[...]).astype(BF16)
    q = _dot(cq, wq_ref[...])
    qscale = LOG2E * float(A_NOPE + A_ROPE) ** -0.5
    for h in range(A_HEADS):
        lo = h * A_QBLK
        q_ref[:, lo:lo + A_NOPE] = (q[:, lo:lo + A_NOPE] * qscale).astype(BF16)
        pr = q[:, lo + A_NOPE:lo + A_QBLK] * cs
        pr = pr + pltpu.roll(pr, A_ROPE, 1)
        q_ref[:, lo + A_NOPE:lo + A_QBLK] = (pr * qscale).astype(BF16)
    ckv = (_rms(z[:, A_QLORA:A_QLORA + A_KVLORA]) * gkv_ref[...]).astype(BF16)
    kv = _dot(ckv, wkv_ref[...])
    kn_ref[...] = kv[:, :A_WIDTH].astype(BF16)
    v_ref[...] = kv[:, A_WIDTH:].astype(BF16)
    o = A_QLORA + A_KVLORA
    kr = z[:, o:o + 2 * A_ROPE]
    lane = lax.broadcasted_iota(jnp.int32, kr.shape, 1)
    half = A_ROPE // 2
    swapped = jnp.where(lane < half, -pltpu.roll(kr, 2 * A_ROPE - half, 1), pltpu.roll(kr, half, 1))
    roped = kr * cs + swapped * pltpu.roll(cs, A_ROPE, 1)
    kr_ref[...] = jnp.where(lane < A_ROPE, roped, 0.0).astype(BF16)


def _latent(z, g_q, w_uq, g_kv, w_ukv, cs_tab, seq, tm=1024):
    t = z.shape[0]
    per_b = seq // tm
    const = lambda m: (0, 0)
    return pl.pallas_call(
        _latent_kernel,
        grid=(t // tm,),
        in_specs=[
            pl.BlockSpec((tm, LATENT_W), lambda m: (m, 0)),
            pl.BlockSpec((1, A_QLORA), const),
            pl.BlockSpec(w_uq.shape, const),
            pl.BlockSpec((1, A_KVLORA), const),
            pl.BlockSpec(w_ukv.shape, const),
            pl.BlockSpec((tm, 2 * A_ROPE), lambda m: (m % per_b, 0)),
        ],
        out_specs=[
            pl.BlockSpec((tm, A_HEADS * A_QBLK), lambda m: (m, 0)),
            pl.BlockSpec((tm, A_WIDTH), lambda m: (m, 0)),
            pl.BlockSpec((tm, A_WIDTH), lambda m: (m, 0)),
            pl.BlockSpec((tm, 2 * A_ROPE), lambda m: (m, 0)),
        ],
        out_shape=[
            jax.ShapeDtypeStruct((t, A_HEADS * A_QBLK), BF16),
            jax.ShapeDtypeStruct((t, A_WIDTH), BF16),
            jax.ShapeDtypeStruct((t, A_WIDTH), BF16),
            jax.ShapeDtypeStruct((t, 2 * A_ROPE), BF16),
        ],
        compiler_params=_params("arbitrary"),
        name="latent",
    )(z, g_q, w_uq, g_kv, w_ukv, cs_tab)


def _with_ones_column(v):
    lane = lax.broadcasted_iota(jnp.int32, v.shape, 1)
    return jnp.concatenate([v, jnp.where(lane == 0, 1.0, 0.0).astype(v.dtype)], axis=1)


def _softmax_pv(s_parts, v_parts):
    m = functools.reduce(jnp.maximum, [jnp.max(s, axis=-1, keepdims=True) for s in s_parts])
    acc = functools.reduce(
        jnp.add, [_dot(jnp.exp2(s - m).astype(BF16), v) for s, v in zip(s_parts, v_parts)])
    width = acc.shape[1] // 2
    return acc[:, :width] / acc[:, width:width + 1]


def _chunk_causal_mask(n):
    r = lax.broadcasted_iota(jnp.int32, (n, n), 0) // CHUNK
    c = lax.broadcasted_iota(jnp.int32, (n, n), 1) // CHUNK
    return c <= r


def _mla_attn_kernel(q_ref, kn_ref, kr_ref, v_ref, o_ref):
    seq = q_ref.shape[0]
    kr = kr_ref[...]
    mask = _chunk_causal_mask(ATT_TQ)
    heads = [(jnp.concatenate([kn_ref[:, hh * A_NOPE:(hh + 1) * A_NOPE], kr], axis=1),
              _with_ones_column(v_ref[:, hh * A_VDIM:(hh + 1) * A_VDIM]))
             for hh in range(HEADS_PER_STEP)]
    for i in reversed(range(seq // ATT_TQ)):
        lo = i * ATT_TQ
        for hh, (k, v) in enumerate(heads):
            q = q_ref[lo:lo + ATT_TQ, hh * A_QBLK:(hh + 1) * A_QBLK]
            s_diag = jnp.where(mask, _dot_nt(q, k[lo:lo + ATT_TQ]), -jnp.inf)
            s_parts, v_parts = [s_diag], [v[lo:lo + ATT_TQ]]
            if i > 0:
                s_parts.append(_dot_nt(q, k[:lo]))
                v_parts.append(v[:lo])
            o_ref[lo:lo + ATT_TQ, hh * A_VDIM:(hh + 1) * A_VDIM] = (
                _softmax_pv(s_parts, v_parts).astype(BF16))


def _mla_attention(q, kn, kr, v):
    bsz, seq, _ = q.shape
    hp = HEADS_PER_STEP
    group = lambda b, g: (b, 0, g)
    return pl.pallas_call(
        _mla_attn_kernel,
        grid=(bsz, A_HEADS // hp),
        in_specs=[
            pl.BlockSpec((None, seq, hp * A_QBLK), group),
            pl.BlockSpec((None, seq, hp * A_NOPE), group),
            pl.BlockSpec((None, seq, 2 * A_ROPE), lambda b, g: (b, 0, 0)),
            pl.BlockSpec((None, seq, hp * A_VDIM), group),
        ],
        out_specs=pl.BlockSpec((None, seq, hp * A_VDIM), group),
        out_shape=jax.ShapeDtypeStruct((bsz, seq, A_WIDTH), BF16),
        compiler_params=_params("arbitrary", "arbitrary"),
        name="mla_attention",
    )(q, kn, kr, v)


def _band_attn_kernel(q_ref, k_ref, v_ref, e_ref, o_ref, bias_ref):
    seq = q_ref.shape[0]
    scale = LOG2E * float(B_HDIM) ** -0.5

    @pl.when(pl.program_id(1) == 0)
    def _():
        for hh in range(HEADS_PER_STEP):
            e = jnp.broadcast_to(e_ref[hh] * LOG2E, (ATT_TQ, e_ref.shape[2]))
            toep = pltpu.roll(e, 0, 1, stride=1, stride_axis=0)[:, :BAND_KEYS]
            qc = lax.broadcasted_iota(jnp.int32, toep.shape, 0) // CHUNK
            kc = lax.broadcasted_iota(jnp.int32, toep.shape, 1) // CHUNK - B_PREV
            bias_ref[hh] = jnp.where((kc <= qc) & (kc >= qc - B_PREV), toep, -jnp.inf)

    vs = [_with_ones_column(v_ref[:, hh * B_HDIM:(hh + 1) * B_HDIM])
          for hh in range(HEADS_PER_STEP)]
    for i in reversed(range(seq // ATT_TQ)):
        lo = i * ATT_TQ
        klo = max(0, lo - B_PREV * CHUNK)
        nk = lo + ATT_TQ - klo
        for hh, v in enumerate(vs):
            cols = slice(hh * B_HDIM, (hh + 1) * B_HDIM)
            q = (q_ref[lo:lo + ATT_TQ, cols].astype(F32) * scale).astype(BF16)
            s = _dot_nt(q, k_ref[klo:klo + nk, cols]) + bias_ref[hh, :, BAND_KEYS - nk:]
            o_ref[lo:lo + ATT_TQ, cols] = _softmax_pv([s], [v[klo:klo + nk]]).astype(BF16)


def _band_attention(z3, e_tab):
    bsz, seq, _ = z3.shape
    hp = HEADS_PER_STEP
    nb = LATENT_W // (hp * B_HDIM)
    ng = B_HEADS // hp

    def col(off):
        return pl.BlockSpec((None, seq, hp * B_HDIM), lambda g, b: (b, 0, off + g))

    return pl.pallas_call(
        _band_attn_kernel,
        grid=(ng, bsz),
        in_specs=[
            col(nb), col(nb + ng), col(nb + 2 * ng),
            pl.BlockSpec((hp, 1, e_tab.shape[2]), lambda g, b: (g, 0, 0)),
        ],
        out_specs=pl.BlockSpec((None, seq, hp * B_HDIM), lambda g, b: (b, 0, g)),
        out_shape=jax.ShapeDtypeStruct((bsz, seq, B_WIDTH), BF16),
        scratch_shapes=[pltpu.VMEM((hp, ATT_TQ, BAND_KEYS), F32)],
        compiler_params=_params("arbitrary", "arbitrary"),
        name="band_attention",
    )(z3, z3, z3, e_tab)


def _band_bias_row(rel_table):
    h = rel_table.shape[0]
    width = 2 * (B_PREV * CHUNK)
    n_hi = B_PREV * CHUNK - REL_CLIP
    n_lo = width - n_hi - (2 * REL_CLIP + 1) - (CHUNK - 1)
    hi = rel_table[:, 2 * REL_CLIP:]
    lo = rel_table[:, :1]
    e = jnp.concatenate([jnp.broadcast_to(hi, (h, n_hi)), rel_table[:, ::-1],
                         jnp.broadcast_to(lo, (h, n_lo)),
                         jnp.broadcast_to(hi, (h, CHUNK - 1))], axis=1)
    return e.astype(F32)[:, None, :]


PIPE_LAG = 1


def _finish(y_fn, refs, emit_h):
    x_ref, w_ref, gpost_ref, mgate_ref = refs[:4]
    o_ref = refs[7 if emit_h else 4]
    wb_ref, r0_ref, r1_ref = refs[-3:]
    i = pl.program_id(0)

    @pl.when(i == 0)
    def _():
        wb_ref[...] = w_ref[...].astype(BF16)
        r1_ref[...] = jnp.zeros_like(r1_ref)

    def step(r_prev, r_cur):
        xn = x_ref[...] + _rms(r_prev[...]) * (mgate_ref[...] * gpost_ref[...])
        o_ref[...] = xn
        if emit_h:
            refs[8][...] = _adaln(xn, *refs[4:7])
        r_cur[...] = _dot(y_fn().astype(BF16), wb_ref[...])

    pl.when(i % 2 == 0)(lambda: step(r1_ref, r0_ref))
    pl.when(i % 2 == 1)(lambda: step(r0_ref, r1_ref))


def _finish_specs(x2, w_stack, layer, g_post, mgate, nxt, seq, tm):
    t, d = x2.shape
    per_b = seq // tm
    prev = lambda i: jnp.maximum(i - PIPE_LAG, 0)
    row = pl.BlockSpec((tm, d), lambda i: (prev(i), 0))
    vec = pl.BlockSpec((1, d), lambda i: (0, 0))
    bvec = pl.BlockSpec((None, 1, d), lambda i: (prev(i) // per_b, 0, 0))
    whole = pl.BlockSpec((None, d, d), lambda i: (layer, 0, 0), pipeline_mode=pl.Buffered(1))
    args, in_specs = [x2, w_stack, g_post, mgate], [row, whole, vec, bvec]
    out_specs, out_shape = [row], [jax.ShapeDtypeStruct((t, d), F32)]
    if nxt is not None:
        args += list(nxt)
        in_specs += [vec, bvec, bvec]
        out_specs.append(row)
        out_shape.append(jax.ShapeDtypeStruct((t, d), BF16))
    scratch = [pltpu.VMEM((d, d), BF16), pltpu.VMEM((tm, d), F32), pltpu.VMEM((tm, d), F32)]
    return args, in_specs, out_specs, out_shape, scratch


def _outproj_even_kernel(emit_h, oa_ref, ob_ref, gate_ref, *refs):
    def y_fn():
        o = jnp.concatenate([oa_ref[...], ob_ref[...]], axis=1).astype(F32)
        return o * _silu(gate_ref[...].astype(F32))

    _finish(y_fn, refs, emit_h)


def _outproj_even(oa, ob, z, x2, w_stack, layer, g_post, mgate, nxt, seq, tm=256):
    t, d = x2.shape
    n_tiles = t // tm
    cur = lambda i: jnp.minimum(i, n_tiles - 1)
    args, in_specs, out_specs, out_shape, scratch = _finish_specs(
        x2, w_stack, layer, g_post, mgate, nxt, seq, tm)
    return pl.pallas_call(
        functools.partial(_outproj_even_kernel, nxt is not None),
        grid=(n_tiles + PIPE_LAG,),
        in_specs=[
            pl.BlockSpec((tm, A_WIDTH), lambda i: (cur(i), 0)),
            pl.BlockSpec((tm, B_WIDTH), lambda i: (cur(i), 0)),
            pl.BlockSpec((tm, d), lambda i: (cur(i), 2)),
        ] + in_specs,
        out_specs=out_specs,
        out_shape=out_shape,
        scratch_shapes=scratch,
        compiler_params=_params("arbitrary"),
        name="outproj_even",
    )(oa, ob, z, *args)


def _sgu_kernel(emit_h, u_ref, v_ref, gate_ref, lng_ref, lnb_ref, ws_ref, bs_ref, *refs):
    sv_ref, refs = refs[-1], refs[:-1]
    tm, d = sv_ref.shape
    dg = d // SG_GROUPS

    def y_fn():
        v = v_ref[...].astype(F32)
        mu = jnp.mean(v, axis=-1, keepdims=True)
        vc = v - mu
        var = jnp.mean(vc * vc, axis=-1, keepdims=True)
        vn = ((vc * lax.rsqrt(var + EPS)) * lng_ref[...] + lnb_ref[...]).astype(BF16)
        cpos_r = lax.broadcasted_iota(jnp.int32, (SG_LEN, SG_LEN), 0) // CHUNK
        cpos_c = lax.broadcasted_iota(jnp.int32, (SG_LEN, SG_LEN), 1) // CHUNK
        mask = cpos_c <= cpos_r
        for g in range(SG_GROUPS):
            ws = jnp.where(mask, ws_ref[g], 0.0).astype(BF16)
            bs = bs_ref[:, g:g + 1]
            for n in range(tm // SG_LEN):
                blk = vn[n * SG_LEN:(n + 1) * SG_LEN, g * dg:(g + 1) * dg]
                sv_ref[n * SG_LEN:(n + 1) * SG_LEN, g * dg:(g + 1) * dg] = _dot(ws, blk) + bs
        return u_ref[...].astype(F32) * sv_ref[...] * _silu(gate_ref[...].astype(F32))

    _finish(y_fn, refs, emit_h)


def _sgu_outproj(z, x2, ln_g, ln_b, w_s, b_s_t, w_stack, layer, g_post, mgate, nxt, seq, tm=256):
    t, d = x2.shape
    n_tiles = t // tm
    cur = lambda i: jnp.minimum(i, n_tiles - 1)
    const2 = lambda i: (0, 0)
    args, in_specs, out_specs, out_shape, scratch = _finish_specs(
        x2, w_stack, layer, g_post, mgate, nxt, seq, tm)
    return pl.pallas_call(
        functools.partial(_sgu_kernel, nxt is not None),
        grid=(n_tiles + PIPE_LAG,),
        in_specs=[
            pl.BlockSpec((tm, d), lambda i: (cur(i), 0)),
            pl.BlockSpec((tm, d), lambda i: (cur(i), 1)),
            pl.BlockSpec((tm, d), lambda i: (cur(i), 2)),
            pl.BlockSpec((1, d), const2),
            pl.BlockSpec((1, d), const2),
            pl.BlockSpec(w_s.shape, lambda i: (0, 0, 0)),
            pl.BlockSpec(b_s_t.shape, const2),
        ] + in_specs,
        out_specs=out_specs,
        out_shape=out_shape,
        scratch_shapes=scratch + [pltpu.VMEM((tm, d), F32)],
        compiler_params=_params("arbitrary"),
        name="sgu_outproj",
    )(z, z, z, ln_g, ln_b, w_s, b_s_t, *args)


def _half_swap(w):
    half = w.shape[-1] // 2
    return jnp.concatenate([-w[..., half:], w[..., :half]], axis=-1)


def _prep_latent_weights(w_uq, w_ukv):
    wq = w_uq.reshape(A_QLORA, A_HEADS, A_NOPE + A_ROPE)
    wq_rope = wq[..., A_NOPE:]
    wq_p = jnp.concatenate([wq, _half_swap(wq_rope)], axis=-1).reshape(A_QLORA, A_HEADS * A_QBLK)
    wkv = w_ukv.reshape(A_KVLORA, A_HEADS, A_NOPE + A_VDIM)
    wkv_p = jnp.concatenate([wkv[..., :A_NOPE].reshape(A_KVLORA, A_WIDTH),
                             wkv[..., A_NOPE:].reshape(A_KVLORA, A_WIDTH)], axis=1)
    return wq_p.astype(BF16), wkv_p.astype(BF16)


def _rope_table(seq):
    half = A_ROPE // 2
    pos = jnp.arange(seq, dtype=F32)
    freqs = ROPE_THETA ** (-jnp.arange(half, dtype=F32) / half)
    ang = pos[:, None] * freqs[None, :]
    cos, sin = jnp.cos(ang), jnp.sin(ang)
    return jnp.concatenate([cos, cos, sin, sin], axis=1)


def kernel(x, c, w_mod, b_mod, g_pre, g_post, ab_w_in, a_g_q, a_w_uq, a_g_kv, a_w_ukv,
           b_rel_bias, ab_w_out, sg_w_in, sg_ln_g, sg_ln_b, sg_w_s, sg_b_s, sg_w_out):
    bsz, seq, d = x.shape
    t = bsz * seq
    mod = _modulation(c, w_mod, b_mod)
    cs_tab = _rope_table(seq)
    x2 = x.reshape(t, d)

    def adaln_args(l):
        return g_pre[l][None], mod[l, :, None, d:2 * d], mod[l, :, None, :d]

    ab_w_in_t = jnp.swapaxes(ab_w_in, 1, 2)
    h = _prenorm(x2, *adaln_args(0), seq)
    for l in range(DEPTH):
        mgate = mod[l, :, None, 2 * d:]
        nxt = adaln_args(l + 1) if l + 1 < DEPTH else None
        i = l // 2
        if l % 2 == 0:
            w_uq, w_ukv = _prep_latent_weights(a_w_uq[i], a_w_ukv[i])
            z = _inproj(h, ab_w_in_t, i, Z_W, transposed=True,
                        carry=LATENT_W - (A_QLORA + A_KVLORA + A_ROPE))
            q, kn, v, kr = _latent(z, a_g_q[i][None], w_uq, a_g_kv[i][None], w_ukv, cs_tab, seq)
            oa = _mla_attention(q.reshape(bsz, seq, -1), kn.reshape(bsz, seq, -1),
                                kr.reshape(bsz, seq, -1), v.reshape(bsz, seq, -1))
            ob = _band_attention(z.reshape(bsz, seq, Z_W), _band_bias_row(b_rel_bias[i]))
            out = _outproj_even(oa.reshape(t, A_WIDTH), ob.reshape(t, B_WIDTH), z, x2,
                                ab_w_out, i, g_post[l][None], mgate, nxt, seq)
        else:
            z = _inproj(h, sg_w_in, i, sg_w_in.shape[2])
            out = _sgu_outproj(z, x2, sg_ln_g[i][None], sg_ln_b[i][None], sg_w_s[i],
                               sg_b_s[i].T, sg_w_out, i, g_post[l][None], mgate, nxt, seq)
        x2, h = out if nxt is not None else (out[0], None)
    return x2.reshape(bsz, seq, d)
```

```python
import functools

import jax
import jax.numpy as jnp
from jax import lax
from jax.experimental import pallas as pl
from jax.experimental.pallas import tpu as pltpu

D_MODEL = 2048
DEPTH = 4
CHUNK = 64
EPS = 1e-6
A_HEADS = 8
A_NOPE = 128
A_ROPE = 64
A_VDIM = 128
A_QLORA = 512
A_KVLORA = 256
ROPE_THETA = 10000.0
B_HEADS = 8
B_HDIM = 128
B_PREV = 8
REL_CLIP = 128
SG_GROUPS = 8
SG_LEN = 128
B_WIDTH = B_HEADS * B_HDIM
A_WIDTH = A_HEADS * A_VDIM

VMEM_LIMIT_BYTES = 56 * 1024 * 1024

SUBLANES = 8

LATENT_W = 1024
Z_W = 6144
A_QBLK = 256
ATT_TQ = 256
HEADS_PER_STEP = 4
BAND_TQ = 256
BAND_KEYS = BAND_TQ + B_PREV * CHUNK

BF16 = jnp.bfloat16
F32 = jnp.float32
LOG2E = 1.4426950408889634


def _params(*sem):
    return pltpu.CompilerParams(dimension_semantics=sem, vmem_limit_bytes=VMEM_LIMIT_BYTES)


def _dot(a, b):
    return jnp.dot(a, b, preferred_element_type=F32)


def _dot_nt(a, b):
    return lax.dot_general(a, b, (((1,), (1,)), ((), ())), preferred_element_type=F32)


def _rms(x):
    return x * lax.rsqrt(jnp.mean(x * x, axis=-1, keepdims=True) + EPS)


def _silu(x):
    return x * jax.nn.sigmoid(x)


def _mod_kernel(c_ref, w_ref, b_ref, o_ref):
    cs = _silu(c_ref[...]).astype(BF16)
    o_ref[...] = _dot(cs, w_ref[...].astype(BF16)) + b_ref[...]


def _modulation(c, w_mod, b_mod3, tn=512):
    _, d, n = w_mod.shape
    bsz = c.shape[0]
    return pl.pallas_call(
        _mod_kernel,
        grid=(1, n // tn),
        in_specs=[
            pl.BlockSpec((bsz, d), lambda l, j: (0, 0)),
            pl.BlockSpec((None, d, tn), lambda l, j: (l, 0, j)),
            pl.BlockSpec((None, 1, tn), lambda l, j: (l, 0, j)),
        ],
        out_specs=pl.BlockSpec((None, bsz, tn), lambda l, j: (l, 0, j)),
        out_shape=jax.ShapeDtypeStruct((1, bsz, n), F32),
        compiler_params=_params("arbitrary", "arbitrary"),
        name="modulation",
    )(c, w_mod, b_mod3)


def _adaln(x, g_ref, scale_ref, shift_ref):
    gain = g_ref[...] * (1.0 + scale_ref[...])
    return (_rms(x) * gain + shift_ref[...]).astype(BF16)


def _prenorm_kernel(x_ref, g_ref, scale_ref, shift_ref, h_ref):
    h_ref[...] = _adaln(x_ref[...], g_ref, scale_ref, shift_ref)


def _prenorm(x2, g_pre, scale, shift, seq, tm=1024):
    t, d = x2.shape
    per_b = seq // tm
    vec = pl.BlockSpec((None, 1, d), lambda m: (m // per_b, 0, 0))
    return pl.pallas_call(
        _prenorm_kernel,
        grid=(t // tm,),
        in_specs=[pl.BlockSpec((tm, d), lambda m: (m, 0)),
                  pl.BlockSpec((1, d), lambda m: (0, 0)), vec, vec],
        out_specs=pl.BlockSpec((tm, d), lambda m: (m, 0)),
        out_shape=jax.ShapeDtypeStruct((t, d), BF16),
        compiler_params=_params("arbitrary"),
        name="prenorm",
    )(x2, g_pre, scale, shift)


def _inproj_kernel(transposed, h_ref, w_ref, o_ref, wb_ref):
    @pl.when(pl.program_id(1) == 0)
    def _():
        wb_ref[...] = (w_ref[0].T if transposed else w_ref[...]).astype(BF16)

    o_ref[...] = _dot(h_ref[...], wb_ref[...]).astype(BF16)


def _inproj(h, w_stack, layer, n_out, transposed=False, carry=0, tm=2048, tn=1024):
    t, d = h.shape
    if transposed:
        w_spec = pl.BlockSpec(
            (pl.Element(1), pl.Element(tn), pl.Element(d)),
            lambda n, m: (layer, pl.multiple_of(jnp.maximum(n * tn - carry, 0), SUBLANES), 0))
    else:
        assert carry == 0
        w_spec = pl.BlockSpec((None, d, tn), lambda n, m: (layer, 0, n))
    return pl.pallas_call(
        functools.partial(_inproj_kernel, transposed),
        grid=(n_out // tn, t // tm),
        in_specs=[pl.BlockSpec((tm, d), lambda n, m: (m, 0)), w_spec],
        out_specs=pl.BlockSpec((tm, tn), lambda n, m: (m, n)),
        out_shape=jax.ShapeDtypeStruct((t, n_out), BF16),
        scratch_shapes=[pltpu.VMEM((d, tn), BF16)],
        compiler_params=_params("arbitrary", "arbitrary"),
        name="inproj",
    )(h, w_stack)


def _latent_kernel(z_ref, gq_ref, wq_ref, gkv_ref, wkv_ref, cs_ref,
                   q_ref, kn_ref, v_ref, kr_ref):
    z = z_ref[...].astype(F32)
    cs = cs_ref[...]
    cq = (_rms(z[:, :A_QLORA]) * gq_ref[...]).astype(BF16)
    q = _dot(cq, wq_ref[...])
    qscale = LOG2E * float(A_NOPE + A_ROPE) ** -0.5
    for h in range(A_HEADS):
        lo = h * A_QBLK
        q_ref[:, lo:lo + A_NOPE] = (q[:, lo:lo + A_NOPE] * qscale).astype(BF16)
        pr = q[:, lo + A_NOPE:lo + A_QBLK] * cs
        pr = pr + pltpu.roll(pr, A_ROPE, 1)
        q_ref[:, lo + A_NOPE:lo + A_QBLK] = (pr * qscale).astype(BF16)
    ckv = (_rms(z[:, A_QLORA:A_QLORA + A_KVLORA]) * gkv_ref[...]).astype(BF16)
    kv = _dot(ckv, wkv_ref[...])
    kn_ref[...] = kv[:, :A_WIDTH].astype(BF16)
    v_ref[...] = kv[:, A_WIDTH:].astype(BF16)
    o = A_QLORA + A_KVLORA
    kr = z[:, o:o + 2 * A_ROPE]
    lane = lax.broadcasted_iota(jnp.int32, kr.shape, 1)
    half = A_ROPE // 2
    swapped = jnp.where(lane < half, -pltpu.roll(kr, 2 * A_ROPE - half, 1), pltpu.roll(kr, half, 1))
    roped = kr * cs + swapped * pltpu.roll(cs, A_ROPE, 1)
    kr_ref[...] = jnp.where(lane < A_ROPE, roped, 0.0).astype(BF16)


def _latent(z, g_q, w_uq, g_kv, w_ukv, cs_tab, seq, tm=1024):
    t = z.shape[0]
    per_b = seq // tm
    const = lambda m: (0, 0)
    return pl.pallas_call(
        _latent_kernel,
        grid=(t // tm,),
        in_specs=[
            pl.BlockSpec((tm, LATENT_W), lambda m: (m, 0)),
            pl.BlockSpec((1, A_QLORA), const),
            pl.BlockSpec(w_uq.shape, const),
            pl.BlockSpec((1, A_KVLORA), const),
            pl.BlockSpec(w_ukv.shape, const),
            pl.BlockSpec((tm, 2 * A_ROPE), lambda m: (m % per_b, 0)),
        ],
        out_specs=[
            pl.BlockSpec((tm, A_HEADS * A_QBLK), lambda m: (m, 0)),
            pl.BlockSpec((tm, A_WIDTH), lambda m: (m, 0)),
            pl.BlockSpec((tm, A_WIDTH), lambda m: (m, 0)),
            pl.BlockSpec((tm, 2 * A_ROPE), lambda m: (m, 0)),
        ],
        out_shape=[
            jax.ShapeDtypeStruct((t, A_HEADS * A_QBLK), BF16),
            jax.ShapeDtypeStruct((t, A_WIDTH), BF16),
            jax.ShapeDtypeStruct((t, A_WIDTH), BF16),
            jax.ShapeDtypeStruct((t, 2 * A_ROPE), BF16),
        ],
        compiler_params=_params("arbitrary"),
        name="latent",
    )(z, g_q, w_uq, g_kv, w_ukv, cs_tab)


def _with_ones_column(v):
    lane = lax.broadcasted_iota(jnp.int32, v.shape, 1)
    return jnp.concatenate([v, jnp.where(lane == 0, 1.0, 0.0).astype(v.dtype)], axis=1)


def _softmax_pv(s_parts, v_parts):
    m = functools.reduce(jnp.maximum, [jnp.max(s, axis=-1, keepdims=True) for s in s_parts])
    acc = functools.reduce(
        jnp.add, [_dot(jnp.exp2(s - m).astype(BF16), v) for s, v in zip(s_parts, v_parts)])
    width = acc.shape[1] // 2
    return acc[:, :width] / acc[:, width:width + 1]


def _chunk_causal_mask(n):
    r = lax.broadcasted_iota(jnp.int32, (n, n), 0) // CHUNK
    c = lax.broadcasted_iota(jnp.int32, (n, n), 1) // CHUNK
    return c <= r


MOD_TILE = 1536


def _mla_attn_kernel(mod_steps, q_ref, kn_ref, kr_ref, v_ref, *refs):
    if mod_steps:
        c_ref, wm_ref, bm_ref, o_ref, mo_ref = refs
        step = pl.program_id(0) * pl.num_programs(1) + pl.program_id(1)
        pl.when(step < mod_steps)(lambda: _mod_kernel(c_ref, wm_ref, bm_ref, mo_ref))
    else:
        o_ref, = refs
    seq = q_ref.shape[0]
    kr = kr_ref[...]
    mask = _chunk_causal_mask(ATT_TQ)
    heads = [(jnp.concatenate([kn_ref[:, hh * A_NOPE:(hh + 1) * A_NOPE], kr], axis=1),
              _with_ones_column(v_ref[:, hh * A_VDIM:(hh + 1) * A_VDIM]))
             for hh in range(HEADS_PER_STEP)]
    for i in reversed(range(seq // ATT_TQ)):
        lo = i * ATT_TQ
        for hh, (k, v) in enumerate(heads):
            q = q_ref[lo:lo + ATT_TQ, hh * A_QBLK:(hh + 1) * A_QBLK]
            s_diag = jnp.where(mask, _dot_nt(q, k[lo:lo + ATT_TQ]), -jnp.inf)
            s_parts, v_parts = [s_diag], [v[lo:lo + ATT_TQ]]
            if i > 0:
                s_parts.append(_dot_nt(q, k[:lo]))
                v_parts.append(v[:lo])
            o_ref[lo:lo + ATT_TQ, hh * A_VDIM:(hh + 1) * A_VDIM] = (
                _softmax_pv(s_parts, v_parts).astype(BF16))


def _mla_attention(q, kn, kr, v, mod_job=None):
    bsz, seq, _ = q.shape
    hp = HEADS_PER_STEP
    ng = A_HEADS // hp
    group = lambda b, g: (b, 0, g)
    in_specs = [
        pl.BlockSpec((None, seq, hp * A_QBLK), group),
        pl.BlockSpec((None, seq, hp * A_NOPE), group),
        pl.BlockSpec((None, seq, 2 * A_ROPE), lambda b, g: (b, 0, 0)),
        pl.BlockSpec((None, seq, hp * A_VDIM), group),
    ]
    out_specs = [pl.BlockSpec((None, seq, hp * A_VDIM), group)]
    out_shape = [jax.ShapeDtypeStruct((bsz, seq, A_WIDTH), BF16)]
    args, mod_steps = [q, kn, kr, v], 0
    if mod_job is not None:
        c, w_mod, b_mod3 = mod_job
        depth, d, n = w_mod.shape
        per_layer = n // MOD_TILE
        mod_steps = (depth - 1) * per_layer
        assert mod_steps <= bsz * ng

        def tile(b, g):
            s = jnp.minimum(b * ng + g, mod_steps - 1)
            return 1 + s // per_layer, s % per_layer

        in_specs += [
            pl.BlockSpec(c.shape, lambda b, g: (0, 0)),
            pl.BlockSpec((None, d, MOD_TILE), lambda b, g: (tile(b, g)[0], 0, tile(b, g)[1])),
            pl.BlockSpec((None, 1, MOD_TILE), lambda b, g: (tile(b, g)[0], 0, tile(b, g)[1])),
        ]
        out_specs.append(pl.BlockSpec((None, c.shape[0], MOD_TILE),
                                      lambda b, g: (tile(b, g)[0] - 1, 0, tile(b, g)[1])))
        out_shape.append(jax.ShapeDtypeStruct((depth - 1, c.shape[0], n), F32))
        args += [c, w_mod, b_mod3]
    return pl.pallas_call(
        functools.partial(_mla_attn_kernel, mod_steps),
        grid=(bsz, ng),
        in_specs=in_specs,
        out_specs=out_specs,
        out_shape=out_shape,
        compiler_params=_params("arbitrary", "arbitrary"),
        name="mla_attention",
    )(*args)


def _band_attn_kernel(q_ref, k_ref, v_ref, e_ref, o_ref, bias_ref):
    seq = q_ref.shape[0]
    scale = LOG2E * float(B_HDIM) ** -0.5

    @pl.when(pl.program_id(1) == 0)
    def _():
        for hh in range(HEADS_PER_STEP):
            e = jnp.broadcast_to(e_ref[hh] * LOG2E, (BAND_TQ, e_ref.shape[2]))
            toep = pltpu.roll(e, 0, 1, stride=1, stride_axis=0)[:, :BAND_KEYS]
            qc = lax.broadcasted_iota(jnp.int32, toep.shape, 0) // CHUNK
            kc = lax.broadcasted_iota(jnp.int32, toep.shape, 1) // CHUNK - B_PREV
            bias_ref[hh] = jnp.where((kc <= qc) & (kc >= qc - B_PREV), toep, -jnp.inf)

    vs = [_with_ones_column(v_ref[:, hh * B_HDIM:(hh + 1) * B_HDIM])
          for hh in range(HEADS_PER_STEP)]
    for i in reversed(range(seq // BAND_TQ)):
        lo = i * BAND_TQ
        klo = max(0, lo - B_PREV * CHUNK)
        nk = lo + BAND_TQ - klo
        for hh, v in enumerate(vs):
            cols = slice(hh * B_HDIM, (hh + 1) * B_HDIM)
            q = (q_ref[lo:lo + BAND_TQ, cols].astype(F32) * scale).astype(BF16)
            s = _dot_nt(q, k_ref[klo:klo + nk, cols]) + bias_ref[hh, :, BAND_KEYS - nk:]
            o_ref[lo:lo + BAND_TQ, cols] = _softmax_pv([s], [v[klo:klo + nk]]).astype(BF16)


def _band_attention(z3, e_tab):
    bsz, seq, _ = z3.shape
    hp = HEADS_PER_STEP
    nb = LATENT_W // (hp * B_HDIM)
    ng = B_HEADS // hp

    def col(off):
        return pl.BlockSpec((None, seq, hp * B_HDIM), lambda g, b: (b, 0, off + g))

    return pl.pallas_call(
        _band_attn_kernel,
        grid=(ng, bsz),
        in_specs=[
            col(nb), col(nb + ng), col(nb + 2 * ng),
            pl.BlockSpec((hp, 1, e_tab.shape[2]), lambda g, b: (g, 0, 0)),
        ],
        out_specs=pl.BlockSpec((None, seq, hp * B_HDIM), lambda g, b: (b, 0, g)),
        out_shape=jax.ShapeDtypeStruct((bsz, seq, B_WIDTH), BF16),
        scratch_shapes=[pltpu.VMEM((hp, BAND_TQ, BAND_KEYS), F32)],
        compiler_params=_params("arbitrary", "arbitrary"),
        name="band_attention",
    )(z3, z3, z3, e_tab)


def _band_bias_row(rel_table):
    h = rel_table.shape[0]
    width = 2 * (B_PREV * CHUNK)
    n_hi = B_PREV * CHUNK - REL_CLIP
    n_lo = width - n_hi - (2 * REL_CLIP + 1) - (CHUNK - 1)
    hi = rel_table[:, 2 * REL_CLIP:]
    lo = rel_table[:, :1]
    e = jnp.concatenate([jnp.broadcast_to(hi, (h, n_hi)), rel_table[:, ::-1],
                         jnp.broadcast_to(lo, (h, n_lo)),
                         jnp.broadcast_to(hi, (h, CHUNK - 1))], axis=1)
    return e.astype(F32)[:, None, :]


PIPE_LAG = 1


def _finish(y_fn, refs, emit_h):
    x_ref, w_ref, gpost_ref, mgate_ref = refs[:4]
    o_ref = refs[7 if emit_h else 4]
    wb_ref, r0_ref, r1_ref = refs[-3:]
    i = pl.program_id(0)

    @pl.when(i == 0)
    def _():
        wb_ref[...] = w_ref[...].astype(BF16)
        r1_ref[...] = jnp.zeros_like(r1_ref)

    def step(r_prev, r_cur):
        xn = x_ref[...] + _rms(r_prev[...]) * (mgate_ref[...] * gpost_ref[...])
        o_ref[...] = xn
        if emit_h:
            refs[8][...] = _adaln(xn, *refs[4:7])
        r_cur[...] = _dot(y_fn().astype(BF16), wb_ref[...])

    pl.when(i % 2 == 0)(lambda: step(r1_ref, r0_ref))
    pl.when(i % 2 == 1)(lambda: step(r0_ref, r1_ref))


def _finish_specs(x2, w_stack, layer, g_post, mgate, nxt, seq, tm):
    t, d = x2.shape
    per_b = seq // tm
    prev = lambda i: jnp.maximum(i - PIPE_LAG, 0)
    row = pl.BlockSpec((tm, d), lambda i: (prev(i), 0))
    vec = pl.BlockSpec((1, d), lambda i: (0, 0))
    bvec = pl.BlockSpec((None, 1, d), lambda i: (prev(i) // per_b, 0, 0))
    whole = pl.BlockSpec((None, d, d), lambda i: (layer, 0, 0), pipeline_mode=pl.Buffered(1))
    args, in_specs = [x2, w_stack, g_post, mgate], [row, whole, vec, bvec]
    out_specs, out_shape = [row], [jax.ShapeDtypeStruct((t, d), F32)]
    if nxt is not None:
        args += list(nxt)
        in_specs += [vec, bvec, bvec]
        out_specs.append(row)
        out_shape.append(jax.ShapeDtypeStruct((t, d), BF16))
    scratch = [pltpu.VMEM((d, d), BF16), pltpu.VMEM((tm, d), F32), pltpu.VMEM((tm, d), F32)]
    return args, in_specs, out_specs, out_shape, scratch


def _outproj_even_kernel(emit_h, oa_ref, ob_ref, gate_ref, *refs):
    def y_fn():
        o = jnp.concatenate([oa_ref[...], ob_ref[...]], axis=1).astype(F32)
        return o * _silu(gate_ref[...].astype(F32))

    _finish(y_fn, refs, emit_h)


def _outproj_even(oa, ob, z, x2, w_stack, layer, g_post, mgate, nxt, seq, tm=256):
    t, d = x2.shape
    n_tiles = t // tm
    cur = lambda i: jnp.minimum(i, n_tiles - 1)
    args, in_specs, out_specs, out_shape, scratch = _finish_specs(
        x2, w_stack, layer, g_post, mgate, nxt, seq, tm)
    return pl.pallas_call(
        functools.partial(_outproj_even_kernel, nxt is not None),
        grid=(n_tiles + PIPE_LAG,),
        in_specs=[
            pl.BlockSpec((tm, A_WIDTH), lambda i: (cur(i), 0)),
            pl.BlockSpec((tm, B_WIDTH), lambda i: (cur(i), 0)),
            pl.BlockSpec((tm, d), lambda i: (cur(i), 2)),
        ] + in_specs,
        out_specs=out_specs,
        out_shape=out_shape,
        scratch_shapes=scratch,
        compiler_params=_params("arbitrary"),
        name="outproj_even",
    )(oa, ob, z, *args)


def _sgu_kernel(emit_h, u_ref, v_ref, gate_ref, lng_ref, lnb_ref, ws_ref, bs_ref, *refs):
    sv_ref, refs = refs[-1], refs[:-1]
    tm, d = sv_ref.shape
    dg = d // SG_GROUPS

    def y_fn():
        v = v_ref[...].astype(F32)
        mu = jnp.mean(v, axis=-1, keepdims=True)
        vc = v - mu
        var = jnp.mean(vc * vc, axis=-1, keepdims=True)
        vn = ((vc * lax.rsqrt(var + EPS)) * lng_ref[...] + lnb_ref[...]).astype(BF16)
        cpos_r = lax.broadcasted_iota(jnp.int32, (SG_LEN, SG_LEN), 0) // CHUNK
        cpos_c = lax.broadcasted_iota(jnp.int32, (SG_LEN, SG_LEN), 1) // CHUNK
        mask = cpos_c <= cpos_r
        for g in range(SG_GROUPS):
            ws = jnp.where(mask, ws_ref[g], 0.0).astype(BF16)
            bs = bs_ref[:, g:g + 1]
            for n in range(tm // SG_LEN):
                blk = vn[n * SG_LEN:(n + 1) * SG_LEN, g * dg:(g + 1) * dg]
                sv_ref[n * SG_LEN:(n + 1) * SG_LEN, g * dg:(g + 1) * dg] = _dot(ws, blk) + bs
        return u_ref[...].astype(F32) * sv_ref[...] * _silu(gate_ref[...].astype(F32))

    _finish(y_fn, refs, emit_h)


def _sgu_outproj(z, x2, ln_g, ln_b, w_s, b_s_t, w_stack, layer, g_post, mgate, nxt, seq, tm=256):
    t, d = x2.shape
    n_tiles = t // tm
    cur = lambda i: jnp.minimum(i, n_tiles - 1)
    const2 = lambda i: (0, 0)
    args, in_specs, out_specs, out_shape, scratch = _finish_specs(
        x2, w_stack, layer, g_post, mgate, nxt, seq, tm)
    return pl.pallas_call(
        functools.partial(_sgu_kernel, nxt is not None),
        grid=(n_tiles + PIPE_LAG,),
        in_specs=[
            pl.BlockSpec((tm, d), lambda i: (cur(i), 0)),
            pl.BlockSpec((tm, d), lambda i: (cur(i), 1)),
            pl.BlockSpec((tm, d), lambda i: (cur(i), 2)),
            pl.BlockSpec((1, d), const2),
            pl.BlockSpec((1, d), const2),
            pl.BlockSpec(w_s.shape, lambda i: (0, 0, 0)),
            pl.BlockSpec(b_s_t.shape, const2),
        ] + in_specs,
        out_specs=out_specs,
        out_shape=out_shape,
        scratch_shapes=scratch + [pltpu.VMEM((tm, d), F32)],
        compiler_params=_params("arbitrary"),
        name="sgu_outproj",
    )(z, z, z, ln_g, ln_b, w_s, b_s_t, *args)


def _half_swap(w):
    half = w.shape[-1] // 2
    return jnp.concatenate([-w[..., half:], w[..., :half]], axis=-1)


def _prep_latent_weights(w_uq, w_ukv):
    wq = w_uq.reshape(A_QLORA, A_HEADS, A_NOPE + A_ROPE)
    wq_rope = wq[..., A_NOPE:]
    wq_p = jnp.concatenate([wq, _half_swap(wq_rope)], axis=-1).reshape(A_QLORA, A_HEADS * A_QBLK)
    wkv = w_ukv.reshape(A_KVLORA, A_HEADS, A_NOPE + A_VDIM)
    wkv_p = jnp.concatenate([wkv[..., :A_NOPE].reshape(A_KVLORA, A_WIDTH),
                             wkv[..., A_NOPE:].reshape(A_KVLORA, A_WIDTH)], axis=1)
    return wq_p.astype(BF16), wkv_p.astype(BF16)


def _rope_table(seq):
    half = A_ROPE // 2
    pos = jnp.arange(seq, dtype=F32)
    freqs = ROPE_THETA ** (-jnp.arange(half, dtype=F32) / half)
    ang = pos[:, None] * freqs[None, :]
    cos, sin = jnp.cos(ang), jnp.sin(ang)
    return jnp.concatenate([cos, cos, sin, sin], axis=1)


def kernel(x, c, w_mod, b_mod, g_pre, g_post, ab_w_in, a_g_q, a_w_uq, a_g_kv, a_w_ukv,
           b_rel_bias, ab_w_out, sg_w_in, sg_ln_g, sg_ln_b, sg_w_s, sg_b_s, sg_w_out):
    bsz, seq, d = x.shape
    t = bsz * seq
    b_mod3 = b_mod.reshape(DEPTH, 1, 3 * d)
    mods = [_modulation(c, w_mod, b_mod3)[0]]
    cs_tab = _rope_table(seq)
    x2 = x.reshape(t, d)

    def adaln_args(l):
        return g_pre[l][None], mods[l][:, None, d:2 * d], mods[l][:, None, :d]

    ab_w_in_t = jnp.swapaxes(ab_w_in, 1, 2)
    h = _prenorm(x2, *adaln_args(0), seq)
    for l in range(DEPTH):
        mgate = mods[l][:, None, 2 * d:]
        i = l // 2
        if l % 2 == 0:
            w_uq, w_ukv = _prep_latent_weights(a_w_uq[i], a_w_ukv[i])
            z = _inproj(h, ab_w_in_t, i, Z_W, transposed=True,
                        carry=LATENT_W - (A_QLORA + A_KVLORA + A_ROPE))
            q, kn, v, kr = _latent(z, a_g_q[i][None], w_uq, a_g_kv[i][None], w_ukv, cs_tab, seq)
            att = _mla_attention(q.reshape(bsz, seq, -1), kn.reshape(bsz, seq, -1),
                                 kr.reshape(bsz, seq, -1), v.reshape(bsz, seq, -1),
                                 mod_job=(c, w_mod, b_mod3) if l == 0 else None)
            oa = att[0]
            if l == 0:
                mods += list(att[1])
            ob = _band_attention(z.reshape(bsz, seq, Z_W), _band_bias_row(b_rel_bias[i]))
            nxt = adaln_args(l + 1) if l + 1 < DEPTH else None
            out = _outproj_even(oa.reshape(t, A_WIDTH), ob.reshape(t, B_WIDTH), z, x2,
                                ab_w_out, i, g_post[l][None], mgate, nxt, seq)
        else:
            nxt = adaln_args(l + 1) if l + 1 < DEPTH else None
            z = _inproj(h, sg_w_in, i, sg_w_in.shape[2])
            out = _sgu_outproj(z, x2, sg_ln_g[i][None], sg_ln_b[i][None], sg_w_s[i],
                               sg_b_s[i].T, sg_w_out, i, g_post[l][None], mgate, nxt, seq)
        x2, h = out if nxt is not None else (out[0], None)
    return x2.reshape(bsz, seq, d)
```

```python
import functools

import jax
import jax.numpy as jnp
from jax import lax
from jax.experimental import pallas as pl
from jax.experimental.pallas import tpu as pltpu

D_MODEL = 2048
DEPTH = 4
CHUNK = 64
EPS = 1e-6
A_HEADS = 8
A_NOPE = 128
A_ROPE = 64
A_VDIM = 128
A_QLORA = 512
A_KVLORA = 256
ROPE_THETA = 10000.0
B_HEADS = 8
B_HDIM = 128
B_PREV = 8
REL_CLIP = 128
SG_GROUPS = 8
SG_LEN = 128
B_WIDTH = B_HEADS * B_HDIM
A_WIDTH = A_HEADS * A_VDIM

VMEM_LIMIT_BYTES = 56 * 1024 * 1024

SUBLANES = 8

LATENT_W = 1024
Z_W = 6144
A_QBLK = 256
ATT_TQ = 256
HEADS_PER_STEP = 4
BAND_TQ = 256
BAND_KEYS = BAND_TQ + B_PREV * CHUNK

BF16 = jnp.bfloat16
F32 = jnp.float32
LOG2E = 1.4426950408889634


def _params(*sem):
    return pltpu.CompilerParams(dimension_semantics=sem, vmem_limit_bytes=VMEM_LIMIT_BYTES)


def _dot(a, b):
    return jnp.dot(a, b, preferred_element_type=F32)


def _dot_nt(a, b):
    return lax.dot_general(a, b, (((1,), (1,)), ((), ())), preferred_element_type=F32)


def _rms(x):
    return x * lax.rsqrt(jnp.mean(x * x, axis=-1, keepdims=True) + EPS)


def _silu(x):
    return x * jax.nn.sigmoid(x)


def _mod_kernel(c_ref, w_ref, b_ref, o_ref):
    cs = _silu(c_ref[...]).astype(BF16)
    o_ref[...] = _dot(cs, w_ref[...].astype(BF16)) + b_ref[...]


def _modulation(c, w_mod, b_mod3, tn=512):
    _, d, n = w_mod.shape
    bsz = c.shape[0]
    return pl.pallas_call(
        _mod_kernel,
        grid=(1, n // tn),
        in_specs=[
            pl.BlockSpec((bsz, d), lambda l, j: (0, 0)),
            pl.BlockSpec((None, d, tn), lambda l, j: (l, 0, j)),
            pl.BlockSpec((None, 1, tn), lambda l, j: (l, 0, j)),
        ],
        out_specs=pl.BlockSpec((None, bsz, tn), lambda l, j: (l, 0, j)),
        out_shape=jax.ShapeDtypeStruct((1, bsz, n), F32),
        compiler_params=_params("arbitrary", "arbitrary"),
        name="modulation",
    )(c, w_mod, b_mod3)


def _adaln(x, g_ref, scale_ref, shift_ref):
    gain = g_ref[...] * (1.0 + scale_ref[...])
    return (_rms(x) * gain + shift_ref[...]).astype(BF16)


def _prenorm_kernel(x_ref, g_ref, scale_ref, shift_ref, h_ref):
    h_ref[...] = _adaln(x_ref[...], g_ref, scale_ref, shift_ref)


def _prenorm(x2, g_pre, scale, shift, seq, tm=1024):
    t, d = x2.shape
    per_b = seq // tm
    vec = pl.BlockSpec((None, 1, d), lambda m: (m // per_b, 0, 0))
    return pl.pallas_call(
        _prenorm_kernel,
        grid=(t // tm,),
        in_specs=[pl.BlockSpec((tm, d), lambda m: (m, 0)),
                  pl.BlockSpec((1, d), lambda m: (0, 0)), vec, vec],
        out_specs=pl.BlockSpec((tm, d), lambda m: (m, 0)),
        out_shape=jax.ShapeDtypeStruct((t, d), BF16),
        compiler_params=_params("arbitrary"),
        name="prenorm",
    )(x2, g_pre, scale, shift)


def _inproj_kernel(transposed, h_ref, w_ref, o_ref, wb_ref):
    @pl.when(pl.program_id(1) == 0)
    def _():
        wb_ref[...] = (w_ref[0].T if transposed else w_ref[...]).astype(BF16)

    o_ref[...] = _dot(h_ref[...], wb_ref[...]).astype(BF16)


def _inproj(h, w_stack, layer, n_out, transposed=False, carry=0, tm=2048, tn=1024):
    t, d = h.shape
    if transposed:
        w_spec = pl.BlockSpec(
            (pl.Element(1), pl.Element(tn), pl.Element(d)),
            lambda n, m: (layer, pl.multiple_of(jnp.maximum(n * tn - carry, 0), SUBLANES), 0))
    else:
        assert carry == 0
        w_spec = pl.BlockSpec((None, d, tn), lambda n, m: (layer, 0, n))
    return pl.pallas_call(
        functools.partial(_inproj_kernel, transposed),
        grid=(n_out // tn, t // tm),
        in_specs=[pl.BlockSpec((tm, d), lambda n, m: (m, 0)), w_spec],
        out_specs=pl.BlockSpec((tm, tn), lambda n, m: (m, n)),
        out_shape=jax.ShapeDtypeStruct((t, n_out), BF16),
        scratch_shapes=[pltpu.VMEM((d, tn), BF16)],
        compiler_params=_params("arbitrary", "arbitrary"),
        name="inproj",
    )(h, w_stack)


def _latent_kernel(z_ref, gq_ref, wq_ref, gkv_ref, wkv_ref, cs_ref,
                   q_ref, kn_ref, v_ref, kr_ref):
    z = z_ref[...].astype(F32)
    cs = cs_ref[...]
    cq = (_rms(z[:, :A_QLORA]) * gq_ref[...]).astype(BF16)
    q = _dot(cq, wq_ref[...])
    qscale = LOG2E * float(A_NOPE + A_ROPE) ** -0.5
    for h in range(A_HEADS):
        lo = h * A_QBLK
        q_ref[:, lo:lo + A_NOPE] = (q[:, lo:lo + A_NOPE] * qscale).astype(BF16)
        pr = q[:, lo + A_NOPE:lo + A_QBLK] * cs
        pr = pr + pltpu.roll(pr, A_ROPE, 1)
        q_ref[:, lo + A_NOPE:lo + A_QBLK] = (pr * qscale).astype(BF16)
    ckv = (_rms(z[:, A_QLORA:A_QLORA + A_KVLORA]) * gkv_ref[...]).astype(BF16)
    kv = _dot(ckv, wkv_ref[...])
    kn_ref[...] = kv[:, :A_WIDTH].astype(BF16)
    v_ref[...] = kv[:, A_WIDTH:].astype(BF16)
    o = A_QLORA + A_KVLORA
    kr = z[:, o:o + 2 * A_ROPE]
    lane = lax.broadcasted_iota(jnp.int32, kr.shape, 1)
    half = A_ROPE // 2
    swapped = jnp.where(lane < half, -pltpu.roll(kr, 2 * A_ROPE - half, 1), pltpu.roll(kr, half, 1))
    roped = kr * cs + swapped * pltpu.roll(cs, A_ROPE, 1)
    kr_ref[...] = jnp.where(lane < A_ROPE, roped, 0.0).astype(BF16)


def _latent(z, g_q, w_uq, g_kv, w_ukv, cs_tab, seq, tm=1024):
    t = z.shape[0]
    per_b = seq // tm
    const = lambda m: (0, 0)
    return pl.pallas_call(
        _latent_kernel,
        grid=(t // tm,),
        in_specs=[
            pl.BlockSpec((tm, LATENT_W), lambda m: (m, 0)),
            pl.BlockSpec((1, A_QLORA), const),
            pl.BlockSpec(w_uq.shape, const),
            pl.BlockSpec((1, A_KVLORA), const),
            pl.BlockSpec(w_ukv.shape, const),
            pl.BlockSpec((tm, 2 * A_ROPE), lambda m: (m % per_b, 0)),
        ],
        out_specs=[
            pl.BlockSpec((tm, A_HEADS * A_QBLK), lambda m: (m, 0)),
            pl.BlockSpec((tm, A_WIDTH), lambda m: (m, 0)),
            pl.BlockSpec((tm, A_WIDTH), lambda m: (m, 0)),
            pl.BlockSpec((tm, 2 * A_ROPE), lambda m: (m, 0)),
        ],
        out_shape=[
            jax.ShapeDtypeStruct((t, A_HEADS * A_QBLK), BF16),
            jax.ShapeDtypeStruct((t, A_WIDTH), BF16),
            jax.ShapeDtypeStruct((t, A_WIDTH), BF16),
            jax.ShapeDtypeStruct((t, 2 * A_ROPE), BF16),
        ],
        compiler_params=_params("arbitrary"),
        name="latent",
    )(z, g_q, w_uq, g_kv, w_ukv, cs_tab)


def _with_ones_column(v):
    lane = lax.broadcasted_iota(jnp.int32, v.shape, 1)
    return jnp.concatenate([v, jnp.where(lane == 0, 1.0, 0.0).astype(v.dtype)], axis=1)


def _softmax_pv(s_parts, v_parts):
    m = functools.reduce(jnp.maximum, [jnp.max(s, axis=-1, keepdims=True) for s in s_parts])
    acc = functools.reduce(
        jnp.add, [_dot(jnp.exp2(s - m).astype(BF16), v) for s, v in zip(s_parts, v_parts)])
    width = acc.shape[1] // 2
    return acc[:, :width] / acc[:, width:width + 1]


def _chunk_causal_mask(n):
    r = lax.broadcasted_iota(jnp.int32, (n, n), 0) // CHUNK
    c = lax.broadcasted_iota(jnp.int32, (n, n), 1) // CHUNK
    return c <= r


MOD_TILE = 1536


def _mla_attn_kernel(mod_steps, q_ref, kn_ref, kr_ref, v_ref, *refs):
    if mod_steps:
        c_ref, wm_ref, bm_ref, o_ref, mo_ref = refs
        step = pl.program_id(0) * pl.num_programs(1) + pl.program_id(1)
        pl.when(step < mod_steps)(lambda: _mod_kernel(c_ref, wm_ref, bm_ref, mo_ref))
    else:
        o_ref, = refs
    seq = q_ref.shape[0]
    kr = kr_ref[...]
    mask = _chunk_causal_mask(ATT_TQ)
    heads = [(jnp.concatenate([kn_ref[:, hh * A_NOPE:(hh + 1) * A_NOPE], kr], axis=1),
              _with_ones_column(v_ref[:, hh * A_VDIM:(hh + 1) * A_VDIM]))
             for hh in range(HEADS_PER_STEP)]
    for i in reversed(range(seq // ATT_TQ)):
        lo = i * ATT_TQ
        for hh, (k, v) in enumerate(heads):
            q = q_ref[lo:lo + ATT_TQ, hh * A_QBLK:(hh + 1) * A_QBLK]
            s_diag = jnp.where(mask, _dot_nt(q, k[lo:lo + ATT_TQ]), -jnp.inf)
            s_parts, v_parts = [s_diag], [v[lo:lo + ATT_TQ]]
            if i > 0:
                s_parts.append(_dot_nt(q, k[:lo]))
                v_parts.append(v[:lo])
            o_ref[lo:lo + ATT_TQ, hh * A_VDIM:(hh + 1) * A_VDIM] = (
                _softmax_pv(s_parts, v_parts).astype(BF16))


def _mla_attention(q, kn, kr, v, mod_job=None):
    bsz, seq, _ = q.shape
    hp = HEADS_PER_STEP
    ng = A_HEADS // hp
    group = lambda b, g: (b, 0, g)
    in_specs = [
        pl.BlockSpec((None, seq, hp * A_QBLK), group),
        pl.BlockSpec((None, seq, hp * A_NOPE), group),
        pl.BlockSpec((None, seq, 2 * A_ROPE), lambda b, g: (b, 0, 0)),
        pl.BlockSpec((None, seq, hp * A_VDIM), group),
    ]
    out_specs = [pl.BlockSpec((None, seq, hp * A_VDIM), group)]
    out_shape = [jax.ShapeDtypeStruct((bsz, seq, A_WIDTH), BF16)]
    args, mod_steps = [q, kn, kr, v], 0
    if mod_job is not None:
        c, w_mod, b_mod3 = mod_job
        depth, d, n = w_mod.shape
        per_layer = n // MOD_TILE
        mod_steps = (depth - 1) * per_layer
        assert mod_steps <= bsz * ng

        def tile(b, g):
            s = jnp.minimum(b * ng + g, mod_steps - 1)
            return 1 + s // per_layer, s % per_layer

        in_specs += [
            pl.BlockSpec(c.shape, lambda b, g: (0, 0)),
            pl.BlockSpec((None, d, MOD_TILE), lambda b, g: (tile(b, g)[0], 0, tile(b, g)[1])),
            pl.BlockSpec((None, 1, MOD_TILE), lambda b, g: (tile(b, g)[0], 0, tile(b, g)[1])),
        ]
        out_specs.append(pl.BlockSpec((None, c.shape[0], MOD_TILE),
                                      lambda b, g: (tile(b, g)[0] - 1, 0, tile(b, g)[1])))
        out_shape.append(jax.ShapeDtypeStruct((depth - 1, c.shape[0], n), F32))
        args += [c, w_mod, b_mod3]
    return pl.pallas_call(
        functools.partial(_mla_attn_kernel, mod_steps),
        grid=(bsz, ng),
        in_specs=in_specs,
        out_specs=out_specs,
        out_shape=out_shape,
        compiler_params=_params("arbitrary", "arbitrary"),
        name="mla_attention",
    )(*args)


def _band_attn_kernel(q_ref, k_ref, v_ref, e_ref, wa_ref, ws_ref, o_ref, wab_ref, wsb_ref, bias_ref):
    wab_ref[...] = wa_ref[...].astype(BF16)
    wsb_ref[...] = ws_ref[...].astype(BF16)
    seq = q_ref.shape[0]
    scale = LOG2E * float(B_HDIM) ** -0.5

    @pl.when(pl.program_id(1) == 0)
    def _():
        for hh in range(HEADS_PER_STEP):
            e = jnp.broadcast_to(e_ref[hh] * LOG2E, (BAND_TQ, e_ref.shape[2]))
            toep = pltpu.roll(e, 0, 1, stride=1, stride_axis=0)[:, :BAND_KEYS]
            qc = lax.broadcasted_iota(jnp.int32, toep.shape, 0) // CHUNK
            kc = lax.broadcasted_iota(jnp.int32, toep.shape, 1) // CHUNK - B_PREV
            bias_ref[hh] = jnp.where((kc <= qc) & (kc >= qc - B_PREV), toep, -jnp.inf)

    vs = [_with_ones_column(v_ref[:, hh * B_HDIM:(hh + 1) * B_HDIM])
          for hh in range(HEADS_PER_STEP)]
    for i in reversed(range(seq // BAND_TQ)):
        lo = i * BAND_TQ
        klo = max(0, lo - B_PREV * CHUNK)
        nk = lo + BAND_TQ - klo
        for hh, v in enumerate(vs):
            cols = slice(hh * B_HDIM, (hh + 1) * B_HDIM)
            q = (q_ref[lo:lo + BAND_TQ, cols].astype(F32) * scale).astype(BF16)
            s = _dot_nt(q, k_ref[klo:klo + nk, cols]) + bias_ref[hh, :, BAND_KEYS - nk:]
            o_ref[lo:lo + BAND_TQ, cols] = _softmax_pv([s], [v[klo:klo + nk]]).astype(BF16)


def _band_attention(z3, e_tab, w_out_a, w_out_s, pair):
    bsz, seq, _ = z3.shape
    hp = HEADS_PER_STEP
    nb = LATENT_W // (hp * B_HDIM)
    ng = B_HEADS // hp
    _, k, n = w_out_a.shape
    slab = k // (ng * bsz)

    def col(off):
        return pl.BlockSpec((None, seq, hp * B_HDIM), lambda g, b: (b, 0, off + g))

    w_in = pl.BlockSpec((None, slab, n), lambda g, b: (pair, g * bsz + b, 0))
    w_out = pl.BlockSpec((slab, n), lambda g, b: (g * bsz + b, 0))
    return pl.pallas_call(
        _band_attn_kernel,
        grid=(ng, bsz),
        in_specs=[
            col(nb), col(nb + ng), col(nb + 2 * ng),
            pl.BlockSpec((hp, 1, e_tab.shape[2]), lambda g, b: (g, 0, 0)),
            w_in, w_in,
        ],
        out_specs=[pl.BlockSpec((None, seq, hp * B_HDIM), lambda g, b: (b, 0, g)), w_out, w_out],
        out_shape=[jax.ShapeDtypeStruct((bsz, seq, B_WIDTH), BF16),
                   jax.ShapeDtypeStruct((k, n), BF16), jax.ShapeDtypeStruct((k, n), BF16)],
        scratch_shapes=[pltpu.VMEM((hp, BAND_TQ, BAND_KEYS), F32)],
        compiler_params=_params("arbitrary", "arbitrary"),
        name="band_attention",
    )(z3, z3, z3, e_tab, w_out_a, w_out_s)


def _band_bias_row(rel_table):
    h = rel_table.shape[0]
    width = 2 * (B_PREV * CHUNK)
    n_hi = B_PREV * CHUNK - REL_CLIP
    n_lo = width - n_hi - (2 * REL_CLIP + 1) - (CHUNK - 1)
    hi = rel_table[:, 2 * REL_CLIP:]
    lo = rel_table[:, :1]
    e = jnp.concatenate([jnp.broadcast_to(hi, (h, n_hi)), rel_table[:, ::-1],
                         jnp.broadcast_to(lo, (h, n_lo)),
                         jnp.broadcast_to(hi, (h, CHUNK - 1))], axis=1)
    return e.astype(F32)[:, None, :]


PIPE_LAG = 1


def _finish(y_fn, refs, emit_h):
    x_ref, wb_ref, gpost_ref, mgate_ref = refs[:4]
    o_ref = refs[7 if emit_h else 4]
    r0_ref, r1_ref = refs[-2:]
    i = pl.program_id(0)

    @pl.when(i == 0)
    def _():
        r1_ref[...] = jnp.zeros_like(r1_ref)

    def step(r_prev, r_cur):
        xn = x_ref[...] + _rms(r_prev[...]) * (mgate_ref[...] * gpost_ref[...])
        o_ref[...] = xn
        if emit_h:
            refs[8][...] = _adaln(xn, *refs[4:7])
        r_cur[...] = _dot(y_fn(r_cur).astype(BF16), wb_ref[...])

    pl.when(i % 2 == 0)(lambda: step(r1_ref, r0_ref))
    pl.when(i % 2 == 1)(lambda: step(r0_ref, r1_ref))


def _finish_specs(x2, w_bf16, g_post, mgate, nxt, seq, tm):
    t, d = x2.shape
    per_b = seq // tm
    prev = lambda i: jnp.maximum(i - PIPE_LAG, 0)
    row = pl.BlockSpec((tm, d), lambda i: (prev(i), 0))
    vec = pl.BlockSpec((1, d), lambda i: (0, 0))
    bvec = pl.BlockSpec((None, 1, d), lambda i: (prev(i) // per_b, 0, 0))
    whole = pl.BlockSpec((d, d), lambda i: (0, 0), pipeline_mode=pl.Buffered(1))
    args, in_specs = [x2, w_bf16, g_post, mgate], [row, whole, vec, bvec]
    out_specs, out_shape = [row], [jax.ShapeDtypeStruct((t, d), F32)]
    if nxt is not None:
        args += list(nxt)
        in_specs += [vec, bvec, bvec]
        out_specs.append(row)
        out_shape.append(jax.ShapeDtypeStruct((t, d), BF16))
    scratch = [pltpu.VMEM((tm, d), F32), pltpu.VMEM((tm, d), F32)]
    return args, in_specs, out_specs, out_shape, scratch


def _outproj_even_kernel(emit_h, oa_ref, ob_ref, gate_ref, *refs):
    def y_fn(_):
        o = jnp.concatenate([oa_ref[...], ob_ref[...]], axis=1).astype(F32)
        return o * _silu(gate_ref[...].astype(F32))

    _finish(y_fn, refs, emit_h)


def _outproj_even(oa, ob, z, x2, w_bf16, g_post, mgate, nxt, seq, tm=512):
    t, d = x2.shape
    n_tiles = t // tm
    cur = lambda i: jnp.minimum(i, n_tiles - 1)
    args, in_specs, out_specs, out_shape, scratch = _finish_specs(
        x2, w_bf16, g_post, mgate, nxt, seq, tm)
    return pl.pallas_call(
        functools.partial(_outproj_even_kernel, nxt is not None),
        grid=(n_tiles + PIPE_LAG,),
        in_specs=[
            pl.BlockSpec((tm, A_WIDTH), lambda i: (cur(i), 0)),
            pl.BlockSpec((tm, B_WIDTH), lambda i: (cur(i), 0)),
            pl.BlockSpec((tm, d), lambda i: (cur(i), 2)),
        ] + in_specs,
        out_specs=out_specs,
        out_shape=out_shape,
        scratch_shapes=scratch,
        compiler_params=_params("arbitrary"),
        name="outproj_even",
    )(oa, ob, z, *args)


def _sgu_kernel(emit_h, u_ref, v_ref, gate_ref, lng_ref, lnb_ref, ws_ref, bs_ref, *refs):
    tm, d = v_ref.shape
    dg = d // SG_GROUPS

    def y_fn(sv_ref):
        v = v_ref[...].astype(F32)
        mu = jnp.mean(v, axis=-1, keepdims=True)
        vc = v - mu
        var = jnp.mean(vc * vc, axis=-1, keepdims=True)
        vn = ((vc * lax.rsqrt(var + EPS)) * lng_ref[...] + lnb_ref[...]).astype(BF16)
        cpos_r = lax.broadcasted_iota(jnp.int32, (SG_LEN, SG_LEN), 0) // CHUNK
        cpos_c = lax.broadcasted_iota(jnp.int32, (SG_LEN, SG_LEN), 1) // CHUNK
        mask = cpos_c <= cpos_r
        for g in range(SG_GROUPS):
            ws = jnp.where(mask, ws_ref[g], 0.0).astype(BF16)
            bs = bs_ref[:, g:g + 1]
            for n in range(tm // SG_LEN):
                blk = vn[n * SG_LEN:(n + 1) * SG_LEN, g * dg:(g + 1) * dg]
                sv_ref[n * SG_LEN:(n + 1) * SG_LEN, g * dg:(g + 1) * dg] = _dot(ws, blk) + bs
        return u_ref[...].astype(F32) * sv_ref[...] * _silu(gate_ref[...].astype(F32))

    _finish(y_fn, refs, emit_h)


def _sgu_outproj(z, x2, ln_g, ln_b, w_s, b_s_t, w_bf16, g_post, mgate, nxt, seq, tm=256):
    t, d = x2.shape
    n_tiles = t // tm
    cur = lambda i: jnp.minimum(i, n_tiles - 1)
    const2 = lambda i: (0, 0)
    args, in_specs, out_specs, out_shape, scratch = _finish_specs(
        x2, w_bf16, g_post, mgate, nxt, seq, tm)
    return pl.pallas_call(
        functools.partial(_sgu_kernel, nxt is not None),
        grid=(n_tiles + PIPE_LAG,),
        in_specs=[
            pl.BlockSpec((tm, d), lambda i: (cur(i), 0)),
            pl.BlockSpec((tm, d), lambda i: (cur(i), 1)),
            pl.BlockSpec((tm, d), lambda i: (cur(i), 2)),
            pl.BlockSpec((1, d), const2),
            pl.BlockSpec((1, d), const2),
            pl.BlockSpec(w_s.shape, lambda i: (0, 0, 0)),
            pl.BlockSpec(b_s_t.shape, const2),
        ] + in_specs,
        out_specs=out_specs,
        out_shape=out_shape,
        scratch_shapes=scratch,
        compiler_params=_params("arbitrary"),
        name="sgu_outproj",
    )(z, z, z, ln_g, ln_b, w_s, b_s_t, *args)


def _half_swap(w):
    half = w.shape[-1] // 2
    return jnp.concatenate([-w[..., half:], w[..., :half]], axis=-1)


def _prep_latent_weights(w_uq, w_ukv):
    wq = w_uq.reshape(A_QLORA, A_HEADS, A_NOPE + A_ROPE)
    wq_rope = wq[..., A_NOPE:]
    wq_p = jnp.concatenate([wq, _half_swap(wq_rope)], axis=-1).reshape(A_QLORA, A_HEADS * A_QBLK)
    wkv = w_ukv.reshape(A_KVLORA, A_HEADS, A_NOPE + A_VDIM)
    wkv_p = jnp.concatenate([wkv[..., :A_NOPE].reshape(A_KVLORA, A_WIDTH),
                             wkv[..., A_NOPE:].reshape(A_KVLORA, A_WIDTH)], axis=1)
    return wq_p.astype(BF16), wkv_p.astype(BF16)


def _rope_table(seq):
    half = A_ROPE // 2
    pos = jnp.arange(seq, dtype=F32)
    freqs = ROPE_THETA ** (-jnp.arange(half, dtype=F32) / half)
    ang = pos[:, None] * freqs[None, :]
    cos, sin = jnp.cos(ang), jnp.sin(ang)
    return jnp.concatenate([cos, cos, sin, sin], axis=1)


def kernel(x, c, w_mod, b_mod, g_pre, g_post, ab_w_in, a_g_q, a_w_uq, a_g_kv, a_w_ukv,
           b_rel_bias, ab_w_out, sg_w_in, sg_ln_g, sg_ln_b, sg_w_s, sg_b_s, sg_w_out):
    bsz, seq, d = x.shape
    t = bsz * seq
    b_mod3 = b_mod.reshape(DEPTH, 1, 3 * d)
    mods = [_modulation(c, w_mod, b_mod3)[0]]
    cs_tab = _rope_table(seq)
    x2 = x.reshape(t, d)

    def adaln_args(l):
        return g_pre[l][None], mods[l][:, None, d:2 * d], mods[l][:, None, :d]

    ab_w_in_t = jnp.swapaxes(ab_w_in, 1, 2)
    h = _prenorm(x2, *adaln_args(0), seq)
    for l in range(DEPTH):
        mgate = mods[l][:, None, 2 * d:]
        i = l // 2
        if l % 2 == 0:
            w_uq, w_ukv = _prep_latent_weights(a_w_uq[i], a_w_ukv[i])
            z = _inproj(h, ab_w_in_t, i, Z_W, transposed=True,
                        carry=LATENT_W - (A_QLORA + A_KVLORA + A_ROPE))
            q, kn, v, kr = _latent(z, a_g_q[i][None], w_uq, a_g_kv[i][None], w_ukv, cs_tab, seq)
            att = _mla_attention(q.reshape(bsz, seq, -1), kn.reshape(bsz, seq, -1),
                                 kr.reshape(bsz, seq, -1), v.reshape(bsz, seq, -1),
                                 mod_job=(c, w_mod, b_mod3) if l == 0 else None)
            oa = att[0]
            if l == 0:
                mods += list(att[1])
            ob, w_out_even, w_out_odd = _band_attention(
                z.reshape(bsz, seq, Z_W), _band_bias_row(b_rel_bias[i]), ab_w_out, sg_w_out, i)
            nxt = adaln_args(l + 1) if l + 1 < DEPTH else None
            out = _outproj_even(oa.reshape(t, A_WIDTH), ob.reshape(t, B_WIDTH), z, x2,
                                w_out_even, g_post[l][None], mgate, nxt, seq)
        else:
            nxt = adaln_args(l + 1) if l + 1 < DEPTH else None
            z = _inproj(h, sg_w_in, i, sg_w_in.shape[2])
            out = _sgu_outproj(z, x2, sg_ln_g[i][None], sg_ln_b[i][None], sg_w_s[i],
                               sg_b_s[i].T, w_out_odd, g_post[l][None], mgate, nxt, seq)
        x2, h = out if nxt is not None else (out[0], None)
    return x2.reshape(bsz, seq, d)
```

```python
import functools

import jax
import jax.numpy as jnp
from jax import lax
from jax.experimental import pallas as pl
from jax.experimental.pallas import tpu as pltpu

D_MODEL = 2048
DEPTH = 4
CHUNK = 64
EPS = 1e-6
A_HEADS = 8
A_NOPE = 128
A_ROPE = 64
A_VDIM = 128
A_QLORA = 512
A_KVLORA = 256
ROPE_THETA = 10000.0
B_HEADS = 8
B_HDIM = 128
B_PREV = 8
REL_CLIP = 128
SG_GROUPS = 8
SG_LEN = 128
B_WIDTH = B_HEADS * B_HDIM
A_WIDTH = A_HEADS * A_VDIM

VMEM_LIMIT_BYTES = 56 * 1024 * 1024

SUBLANES = 8

LATENT_W = 1024
Z_W = 6144
A_QBLK = 256
ATT_TQ = 256
HEADS_PER_STEP = 4
BAND_TQ = 256
BAND_KEYS = BAND_TQ + B_PREV * CHUNK

BF16 = jnp.bfloat16
F32 = jnp.float32
LOG2E = 1.4426950408889634


def _params(*sem):
    return pltpu.CompilerParams(dimension_semantics=sem, vmem_limit_bytes=VMEM_LIMIT_BYTES)


def _dot(a, b):
    return jnp.dot(a, b, preferred_element_type=F32)


def _dot_nt(a, b):
    return lax.dot_general(a, b, (((1,), (1,)), ((), ())), preferred_element_type=F32)


def _rms(x):
    return x * lax.rsqrt(jnp.mean(x * x, axis=-1, keepdims=True) + EPS)


def _silu(x):
    return x * jax.nn.sigmoid(x)


def _mod_kernel(c_ref, w_ref, b_ref, o_ref):
    cs = _silu(c_ref[...]).astype(BF16)
    o_ref[...] = _dot(cs, w_ref[...].astype(BF16)) + b_ref[...]


def _modulation(c, w_mod, b_mod3, tn=512):
    _, d, n = w_mod.shape
    bsz = c.shape[0]
    return pl.pallas_call(
        _mod_kernel,
        grid=(1, n // tn),
        in_specs=[
            pl.BlockSpec((bsz, d), lambda l, j: (0, 0)),
            pl.BlockSpec((None, d, tn), lambda l, j: (l, 0, j)),
            pl.BlockSpec((None, 1, tn), lambda l, j: (l, 0, j)),
        ],
        out_specs=pl.BlockSpec((None, bsz, tn), lambda l, j: (l, 0, j)),
        out_shape=jax.ShapeDtypeStruct((1, bsz, n), F32),
        compiler_params=_params("arbitrary", "arbitrary"),
        name="modulation",
    )(c, w_mod, b_mod3)


def _adaln(x, g_ref, scale_ref, shift_ref):
    gain = g_ref[...] * (1.0 + scale_ref[...])
    return (_rms(x) * gain + shift_ref[...]).astype(BF16)


def _prenorm_kernel(x_ref, g_ref, scale_ref, shift_ref, h_ref):
    h_ref[...] = _adaln(x_ref[...], g_ref, scale_ref, shift_ref)


def _prenorm(x2, g_pre, scale, shift, seq, tm=1024):
    t, d = x2.shape
    per_b = seq // tm
    vec = pl.BlockSpec((None, 1, d), lambda m: (m // per_b, 0, 0))
    return pl.pallas_call(
        _prenorm_kernel,
        grid=(t // tm,),
        in_specs=[pl.BlockSpec((tm, d), lambda m: (m, 0)),
                  pl.BlockSpec((1, d), lambda m: (0, 0)), vec, vec],
        out_specs=pl.BlockSpec((tm, d), lambda m: (m, 0)),
        out_shape=jax.ShapeDtypeStruct((t, d), BF16),
        compiler_params=_params("arbitrary"),
        name="prenorm",
    )(x2, g_pre, scale, shift)


def _inproj_kernel(transposed, h_ref, w_ref, o_ref, wb_ref):
    @pl.when(pl.program_id(1) == 0)
    def _():
        wb_ref[...] = (w_ref[0].T if transposed else w_ref[...]).astype(BF16)

    o_ref[...] = _dot(h_ref[...], wb_ref[...]).astype(BF16)


def _inproj(h, w_stack, layer, n_out, transposed=False, carry=0, tm=2048, tn=1024):
    t, d = h.shape
    if transposed:
        w_spec = pl.BlockSpec(
            (pl.Element(1), pl.Element(tn), pl.Element(d)),
            lambda n, m: (layer, pl.multiple_of(jnp.maximum(n * tn - carry, 0), SUBLANES), 0))
    else:
        assert carry == 0
        w_spec = pl.BlockSpec((None, d, tn), lambda n, m: (layer, 0, n))
    return pl.pallas_call(
        functools.partial(_inproj_kernel, transposed),
        grid=(n_out // tn, t // tm),
        in_specs=[pl.BlockSpec((tm, d), lambda n, m: (m, 0)), w_spec],
        out_specs=pl.BlockSpec((tm, tn), lambda n, m: (m, n)),
        out_shape=jax.ShapeDtypeStruct((t, n_out), BF16),
        scratch_shapes=[pltpu.VMEM((d, tn), BF16)],
        compiler_params=_params("arbitrary", "arbitrary"),
        name="inproj",
    )(h, w_stack)


def _latent_kernel(z_ref, gq_ref, wq_ref, gkv_ref, wkv_ref, cs_ref,
                   q_ref, kn_ref, v_ref, kr_ref):
    z = z_ref[...].astype(F32)
    cs = cs_ref[...]
    cq = (_rms(z[:, :A_QLORA]) * gq_ref[...]).astype(BF16)
    q = _dot(cq, wq_ref[...])
    qscale = LOG2E * float(A_NOPE + A_ROPE) ** -0.5
    for h in range(A_HEADS):
        lo = h * A_QBLK
        q_ref[:, lo:lo + A_NOPE] = (q[:, lo:lo + A_NOPE] * qscale).astype(BF16)
        pr = q[:, lo + A_NOPE:lo + A_QBLK] * cs
        pr = pr + pltpu.roll(pr, A_ROPE, 1)
        q_ref[:, lo + A_NOPE:lo + A_QBLK] = (pr * qscale).astype(BF16)
    ckv = (_rms(z[:, A_QLORA:A_QLORA + A_KVLORA]) * gkv_ref[...]).astype(BF16)
    kv = _dot(ckv, wkv_ref[...])
    kn_ref[...] = kv[:, :A_WIDTH].astype(BF16)
    v_ref[...] = kv[:, A_WIDTH:].astype(BF16)
    o = A_QLORA + A_KVLORA
    kr = z[:, o:o + 2 * A_ROPE]
    lane = lax.broadcasted_iota(jnp.int32, kr.shape, 1)
    half = A_ROPE // 2
    swapped = jnp.where(lane < half, -pltpu.roll(kr, 2 * A_ROPE - half, 1), pltpu.roll(kr, half, 1))
    roped = kr * cs + swapped * pltpu.roll(cs, A_ROPE, 1)
    kr_ref[...] = jnp.where(lane < A_ROPE, roped, 0.0).astype(BF16)


def _latent(z, g_q, w_uq, g_kv, w_ukv, cs_tab, seq, tm=1024):
    t = z.shape[0]
    per_b = seq // tm
    const = lambda m: (0, 0)
    return pl.pallas_call(
        _latent_kernel,
        grid=(t // tm,),
        in_specs=[
            pl.BlockSpec((tm, LATENT_W), lambda m: (m, 0)),
            pl.BlockSpec((1, A_QLORA), const),
            pl.BlockSpec(w_uq.shape, const),
            pl.BlockSpec((1, A_KVLORA), const),
            pl.BlockSpec(w_ukv.shape, const),
            pl.BlockSpec((tm, 2 * A_ROPE), lambda m: (m % per_b, 0)),
        ],
        out_specs=[
            pl.BlockSpec((tm, A_HEADS * A_QBLK), lambda m: (m, 0)),
            pl.BlockSpec((tm, A_WIDTH), lambda m: (m, 0)),
            pl.BlockSpec((tm, A_WIDTH), lambda m: (m, 0)),
            pl.BlockSpec((tm, 2 * A_ROPE), lambda m: (m, 0)),
        ],
        out_shape=[
            jax.ShapeDtypeStruct((t, A_HEADS * A_QBLK), BF16),
            jax.ShapeDtypeStruct((t, A_WIDTH), BF16),
            jax.ShapeDtypeStruct((t, A_WIDTH), BF16),
            jax.ShapeDtypeStruct((t, 2 * A_ROPE), BF16),
        ],
        compiler_params=_params("arbitrary"),
        name="latent",
    )(z, g_q, w_uq, g_kv, w_ukv, cs_tab)


def _with_ones_column(v):
    lane = lax.broadcasted_iota(jnp.int32, v.shape, 1)
    return jnp.concatenate([v, jnp.where(lane == 0, 1.0, 0.0).astype(v.dtype)], axis=1)


def _softmax_pv(s_parts, v_parts):
    m = functools.reduce(jnp.maximum, [jnp.max(s, axis=-1, keepdims=True) for s in s_parts])
    acc = functools.reduce(
        jnp.add, [_dot(jnp.exp2(s - m).astype(BF16), v) for s, v in zip(s_parts, v_parts)])
    width = acc.shape[1] // 2
    return acc[:, :width] / acc[:, width:width + 1]


def _chunk_causal_mask(n):
    r = lax.broadcasted_iota(jnp.int32, (n, n), 0) // CHUNK
    c = lax.broadcasted_iota(jnp.int32, (n, n), 1) // CHUNK
    return c <= r


MOD_TILE = 1536


def _mla_attn_kernel(mod_steps, q_ref, kn_ref, kr_ref, v_ref, *refs):
    if mod_steps:
        c_ref, wm_ref, bm_ref, o_ref, mo_ref = refs
        step = pl.program_id(0) * pl.num_programs(1) + pl.program_id(1)
        pl.when(step < mod_steps)(lambda: _mod_kernel(c_ref, wm_ref, bm_ref, mo_ref))
    else:
        o_ref, = refs
    seq = q_ref.shape[0]
    kr = kr_ref[...]
    mask = _chunk_causal_mask(ATT_TQ)
    heads = [(jnp.concatenate([kn_ref[:, hh * A_NOPE:(hh + 1) * A_NOPE], kr], axis=1),
              _with_ones_column(v_ref[:, hh * A_VDIM:(hh + 1) * A_VDIM]))
             for hh in range(HEADS_PER_STEP)]
    for i in reversed(range(seq // ATT_TQ)):
        lo = i * ATT_TQ
        for hh, (k, v) in enumerate(heads):
            q = q_ref[lo:lo + ATT_TQ, hh * A_QBLK:(hh + 1) * A_QBLK]
            s_diag = jnp.where(mask, _dot_nt(q, k[lo:lo + ATT_TQ]), -jnp.inf)
            s_parts, v_parts = [s_diag], [v[lo:lo + ATT_TQ]]
            if i > 0:
                s_parts.append(_dot_nt(q, k[:lo]))
                v_parts.append(v[:lo])
            o_ref[lo:lo + ATT_TQ, hh * A_VDIM:(hh + 1) * A_VDIM] = (
                _softmax_pv(s_parts, v_parts).astype(BF16))


def _mla_attention(q, kn, kr, v, mod_job=None):
    bsz, seq, _ = q.shape
    hp = HEADS_PER_STEP
    ng = A_HEADS // hp
    group = lambda b, g: (b, 0, g)
    in_specs = [
        pl.BlockSpec((None, seq, hp * A_QBLK), group),
        pl.BlockSpec((None, seq, hp * A_NOPE), group),
        pl.BlockSpec((None, seq, 2 * A_ROPE), lambda b, g: (b, 0, 0)),
        pl.BlockSpec((None, seq, hp * A_VDIM), group),
    ]
    out_specs = [pl.BlockSpec((None, seq, hp * A_VDIM), group)]
    out_shape = [jax.ShapeDtypeStruct((bsz, seq, A_WIDTH), BF16)]
    args, mod_steps = [q, kn, kr, v], 0
    if mod_job is not None:
        c, w_mod, b_mod3 = mod_job
        depth, d, n = w_mod.shape
        per_layer = n // MOD_TILE
        mod_steps = (depth - 1) * per_layer
        assert mod_steps <= bsz * ng

        def tile(b, g):
            s = jnp.minimum(b * ng + g, mod_steps - 1)
            return 1 + s // per_layer, s % per_layer

        in_specs += [
            pl.BlockSpec(c.shape, lambda b, g: (0, 0)),
            pl.BlockSpec((None, d, MOD_TILE), lambda b, g: (tile(b, g)[0], 0, tile(b, g)[1])),
            pl.BlockSpec((None, 1, MOD_TILE), lambda b, g: (tile(b, g)[0], 0, tile(b, g)[1])),
        ]
        out_specs.append(pl.BlockSpec((None, c.shape[0], MOD_TILE),
                                      lambda b, g: (tile(b, g)[0] - 1, 0, tile(b, g)[1])))
        out_shape.append(jax.ShapeDtypeStruct((depth - 1, c.shape[0], n), F32))
        args += [c, w_mod, b_mod3]
    return pl.pallas_call(
        functools.partial(_mla_attn_kernel, mod_steps),
        grid=(bsz, ng),
        in_specs=in_specs,
        out_specs=out_specs,
        out_shape=out_shape,
        compiler_params=_params("arbitrary", "arbitrary"),
        name="mla_attention",
    )(*args)


def _band_attn_kernel(q_ref, k_ref, v_ref, e_ref, wa_ref, ws_ref, o_ref, wab_ref, wsb_ref, bias_ref):
    wab_ref[...] = wa_ref[...].astype(BF16)
    wsb_ref[...] = ws_ref[...].astype(BF16)
    seq = q_ref.shape[0]
    scale = LOG2E * float(B_HDIM) ** -0.5

    @pl.when(pl.program_id(1) == 0)
    def _():
        for hh in range(HEADS_PER_STEP):
            e = jnp.broadcast_to(e_ref[hh] * LOG2E, (BAND_TQ, e_ref.shape[2]))
            toep = pltpu.roll(e, 0, 1, stride=1, stride_axis=0)[:, :BAND_KEYS]
            qc = lax.broadcasted_iota(jnp.int32, toep.shape, 0) // CHUNK
            kc = lax.broadcasted_iota(jnp.int32, toep.shape, 1) // CHUNK - B_PREV
            bias_ref[hh] = jnp.where((kc <= qc) & (kc >= qc - B_PREV), toep, -jnp.inf)

    vs = [_with_ones_column(v_ref[:, hh * B_HDIM:(hh + 1) * B_HDIM])
          for hh in range(HEADS_PER_STEP)]
    for i in reversed(range(seq // BAND_TQ)):
        lo = i * BAND_TQ
        klo = max(0, lo - B_PREV * CHUNK)
        nk = lo + BAND_TQ - klo
        for hh, v in enumerate(vs):
            cols = slice(hh * B_HDIM, (hh + 1) * B_HDIM)
            q = (q_ref[lo:lo + BAND_TQ, cols].astype(F32) * scale).astype(BF16)
            s = _dot_nt(q, k_ref[klo:klo + nk, cols]) + bias_ref[hh, :, BAND_KEYS - nk:]
            o_ref[lo:lo + BAND_TQ, cols] = _softmax_pv([s], [v[klo:klo + nk]]).astype(BF16)


def _band_attention(z3, e_tab, w_out_a, w_out_s, pair):
    bsz, seq, _ = z3.shape
    hp = HEADS_PER_STEP
    nb = LATENT_W // (hp * B_HDIM)
    ng = B_HEADS // hp
    _, k, n = w_out_a.shape
    slab = k // (ng * bsz)

    def col(off):
        return pl.BlockSpec((None, seq, hp * B_HDIM), lambda g, b: (b, 0, off + g))

    w_in = pl.BlockSpec((None, slab, n), lambda g, b: (pair, g * bsz + b, 0))
    w_out = pl.BlockSpec((slab, n), lambda g, b: (g * bsz + b, 0))
    return pl.pallas_call(
        _band_attn_kernel,
        grid=(ng, bsz),
        in_specs=[
            col(nb), col(nb + ng), col(nb + 2 * ng),
            pl.BlockSpec((hp, 1, e_tab.shape[2]), lambda g, b: (g, 0, 0)),
            w_in, w_in,
        ],
        out_specs=[pl.BlockSpec((None, seq, hp * B_HDIM), lambda g, b: (b, 0, g)), w_out, w_out],
        out_shape=[jax.ShapeDtypeStruct((bsz, seq, B_WIDTH), BF16),
                   jax.ShapeDtypeStruct((k, n), BF16), jax.ShapeDtypeStruct((k, n), BF16)],
        scratch_shapes=[pltpu.VMEM((hp, BAND_TQ, BAND_KEYS), F32)],
        compiler_params=_params("arbitrary", "arbitrary"),
        name="band_attention",
    )(z3, z3, z3, e_tab, w_out_a, w_out_s)


def _band_bias_row(rel_table):
    h = rel_table.shape[0]
    width = 2 * (B_PREV * CHUNK)
    n_hi = B_PREV * CHUNK - REL_CLIP
    n_lo = width - n_hi - (2 * REL_CLIP + 1) - (CHUNK - 1)
    hi = rel_table[:, 2 * REL_CLIP:]
    lo = rel_table[:, :1]
    e = jnp.concatenate([jnp.broadcast_to(hi, (h, n_hi)), rel_table[:, ::-1],
                         jnp.broadcast_to(lo, (h, n_lo)),
                         jnp.broadcast_to(hi, (h, CHUNK - 1))], axis=1)
    return e.astype(F32)[:, None, :]


PIPE_LAG = 1


def _finish(y_fn, refs, emit_h):
    x_ref, wb_ref, gpost_ref, mgate_ref = refs[:4]
    o_ref = refs[7 if emit_h else 4]
    r0_ref, r1_ref = refs[-2:]
    i = pl.program_id(0)

    @pl.when(i == 0)
    def _():
        r1_ref[...] = jnp.zeros_like(r1_ref)

    def step(r_prev, r_cur):
        xn = x_ref[...] + _rms(r_prev[...]) * (mgate_ref[...] * gpost_ref[...])
        o_ref[...] = xn
        if emit_h:
            refs[8][...] = _adaln(xn, *refs[4:7])
        r_cur[...] = _dot(y_fn(r_cur).astype(BF16), wb_ref[...])

    pl.when(i % 2 == 0)(lambda: step(r1_ref, r0_ref))
    pl.when(i % 2 == 1)(lambda: step(r0_ref, r1_ref))


def _finish_specs(x2, w_bf16, g_post, mgate, nxt, seq, tm):
    t, d = x2.shape
    per_b = seq // tm
    prev = lambda i: jnp.maximum(i - PIPE_LAG, 0)
    row = pl.BlockSpec((tm, d), lambda i: (prev(i), 0))
    vec = pl.BlockSpec((1, d), lambda i: (0, 0))
    bvec = pl.BlockSpec((None, 1, d), lambda i: (prev(i) // per_b, 0, 0))
    whole = pl.BlockSpec((d, d), lambda i: (0, 0), pipeline_mode=pl.Buffered(1))
    args, in_specs = [x2, w_bf16, g_post, mgate], [row, whole, vec, bvec]
    out_specs, out_shape = [row], [jax.ShapeDtypeStruct((t, d), F32)]
    if nxt is not None:
        args += list(nxt)
        in_specs += [vec, bvec, bvec]
        out_specs.append(row)
        out_shape.append(jax.ShapeDtypeStruct((t, d), BF16))
    scratch = [pltpu.VMEM((tm, d), F32), pltpu.VMEM((tm, d), F32)]
    return args, in_specs, out_specs, out_shape, scratch


def _outproj_even_kernel(emit_h, oa_ref, ob_ref, gate_ref, *refs):
    def y_fn(_):
        o = jnp.concatenate([oa_ref[...], ob_ref[...]], axis=1).astype(F32)
        return o * _silu(gate_ref[...].astype(F32))

    _finish(y_fn, refs, emit_h)


def _outproj_even(oa, ob, z, x2, w_bf16, g_post, mgate, nxt, seq, tm=512):
    t, d = x2.shape
    n_tiles = t // tm
    cur = lambda i: jnp.minimum(i, n_tiles - 1)
    args, in_specs, out_specs, out_shape, scratch = _finish_specs(
        x2, w_bf16, g_post, mgate, nxt, seq, tm)
    return pl.pallas_call(
        functools.partial(_outproj_even_kernel, nxt is not None),
        grid=(n_tiles + PIPE_LAG,),
        in_specs=[
            pl.BlockSpec((tm, A_WIDTH), lambda i: (cur(i), 0)),
            pl.BlockSpec((tm, B_WIDTH), lambda i: (cur(i), 0)),
            pl.BlockSpec((tm, d), lambda i: (cur(i), 2)),
        ] + in_specs,
        out_specs=out_specs,
        out_shape=out_shape,
        scratch_shapes=scratch,
        compiler_params=_params("arbitrary"),
        name="outproj_even",
    )(oa, ob, z, *args)


def _sgu_kernel(emit_h, u_ref, v_ref, gate_ref, lng_ref, lnb_ref, ws_ref, bs_ref, *refs):
    tm, d = v_ref.shape
    dg = d // SG_GROUPS

    def y_fn(sv_ref):
        v = v_ref[...].astype(F32)
        mu = jnp.mean(v, axis=-1, keepdims=True)
        vc = v - mu
        var = jnp.mean(vc * vc, axis=-1, keepdims=True)
        vn = ((vc * lax.rsqrt(var + EPS)) * lng_ref[...] + lnb_ref[...]).astype(BF16)
        cpos_r = lax.broadcasted_iota(jnp.int32, (SG_LEN, SG_LEN), 0) // CHUNK
        cpos_c = lax.broadcasted_iota(jnp.int32, (SG_LEN, SG_LEN), 1) // CHUNK
        mask = cpos_c <= cpos_r
        for g in range(SG_GROUPS):
            ws = jnp.where(mask, ws_ref[g], 0.0).astype(BF16)
            bs = bs_ref[:, g:g + 1]
            for n in range(tm // SG_LEN):
                blk = vn[n * SG_LEN:(n + 1) * SG_LEN, g * dg:(g + 1) * dg]
                sv_ref[n * SG_LEN:(n + 1) * SG_LEN, g * dg:(g + 1) * dg] = _dot(ws, blk) + bs
        return u_ref[...].astype(F32) * sv_ref[...] * _silu(gate_ref[...].astype(F32))

    _finish(y_fn, refs, emit_h)


def _sgu_outproj(z, x2, ln_g, ln_b, w_s, b_s_t, w_bf16, g_post, mgate, nxt, seq, tm=256):
    t, d = x2.shape
    n_tiles = t // tm
    cur = lambda i: jnp.minimum(i, n_tiles - 1)
    const2 = lambda i: (0, 0)
    args, in_specs, out_specs, out_shape, scratch = _finish_specs(
        x2, w_bf16, g_post, mgate, nxt, seq, tm)
    return pl.pallas_call(
        functools.partial(_sgu_kernel, nxt is not None),
        grid=(n_tiles + PIPE_LAG,),
        in_specs=[
            pl.BlockSpec((tm, d), lambda i: (cur(i), 0)),
            pl.BlockSpec((tm, d), lambda i: (cur(i), 1)),
            pl.BlockSpec((tm, d), lambda i: (cur(i), 2)),
            pl.BlockSpec((1, d), const2),
            pl.BlockSpec((1, d), const2),
            pl.BlockSpec(w_s.shape, lambda i: (0, 0, 0)),
            pl.BlockSpec(b_s_t.shape, const2),
        ] + in_specs,
        out_specs=out_specs,
        out_shape=out_shape,
        scratch_shapes=scratch,
        compiler_params=_params("arbitrary"),
        name="sgu_outproj",
    )(z, z, z, ln_g, ln_b, w_s, b_s_t, *args)


def _half_swap(w):
    half = w.shape[-1] // 2
    return jnp.concatenate([-w[..., half:], w[..., :half]], axis=-1)


def _prep_latent_weights(w_uq, w_ukv):
    wq = w_uq.reshape(A_QLORA, A_HEADS, A_NOPE + A_ROPE)
    wq_rope = wq[..., A_NOPE:]
    wq_p = jnp.concatenate([wq, _half_swap(wq_rope)], axis=-1).reshape(A_QLORA, A_HEADS * A_QBLK)
    wkv = w_ukv.reshape(A_KVLORA, A_HEADS, A_NOPE + A_VDIM)
    wkv_p = jnp.concatenate([wkv[..., :A_NOPE].reshape(A_KVLORA, A_WIDTH),
                             wkv[..., A_NOPE:].reshape(A_KVLORA, A_WIDTH)], axis=1)
    return wq_p.astype(BF16), wkv_p.astype(BF16)


def _rope_table(seq):
    half = A_ROPE // 2
    pos = jnp.arange(seq, dtype=F32)
    freqs = ROPE_THETA ** (-jnp.arange(half, dtype=F32) / half)
    ang = pos[:, None] * freqs[None, :]
    cos, sin = jnp.cos(ang), jnp.sin(ang)
    return jnp.concatenate([cos, cos, sin, sin], axis=1)


def kernel(x, c, w_mod, b_mod, g_pre, g_post, ab_w_in, a_g_q, a_w_uq, a_g_kv, a_w_ukv,
           b_rel_bias, ab_w_out, sg_w_in, sg_ln_g, sg_ln_b, sg_w_s, sg_b_s, sg_w_out):
    bsz, seq, d = x.shape
    t = bsz * seq
    assert x.dtype == F32 and d == D_MODEL == A_WIDTH + B_WIDTH and w_mod.shape[0] == DEPTH
    assert ab_w_in.shape[2] + LATENT_W - (A_QLORA + A_KVLORA + A_ROPE) == Z_W
    b_mod3 = b_mod.reshape(DEPTH, 1, 3 * d)
    mods = [_modulation(c, w_mod, b_mod3)[0]]
    cs_tab = _rope_table(seq)
    x2 = x.reshape(t, d)

    def adaln_args(l):
        return g_pre[l][None], mods[l][:, None, d:2 * d], mods[l][:, None, :d]

    ab_w_in_t = jnp.swapaxes(ab_w_in, 1, 2)
    h = _prenorm(x2, *adaln_args(0), seq)
    for l in range(DEPTH):
        mgate = mods[l][:, None, 2 * d:]
        i = l // 2
        if l % 2 == 0:
            w_uq, w_ukv = _prep_latent_weights(a_w_uq[i], a_w_ukv[i])
            z = _inproj(h, ab_w_in_t, i, Z_W, transposed=True,
                        carry=LATENT_W - (A_QLORA + A_KVLORA + A_ROPE))
            q, kn, v, kr = _latent(z, a_g_q[i][None], w_uq, a_g_kv[i][None], w_ukv, cs_tab, seq)
            att = _mla_attention(q.reshape(bsz, seq, -1), kn.reshape(bsz, seq, -1),
                                 kr.reshape(bsz, seq, -1), v.reshape(bsz, seq, -1),
                                 mod_job=(c, w_mod, b_mod3) if l == 0 else None)
            oa = att[0]
            if l == 0:
                mods += list(att[1])
            ob, w_out_even, w_out_odd = _band_attention(
                z.reshape(bsz, seq, Z_W), _band_bias_row(b_rel_bias[i]), ab_w_out, sg_w_out, i)
            nxt = adaln_args(l + 1) if l + 1 < DEPTH else None
            out = _outproj_even(oa.reshape(t, A_WIDTH), ob.reshape(t, B_WIDTH), z, x2,
                                w_out_even, g_post[l][None], mgate, nxt, seq)
        else:
            nxt = adaln_args(l + 1) if l + 1 < DEPTH else None
            z = _inproj(h, sg_w_in, i, sg_w_in.shape[2])
            out = _sgu_outproj(z, x2, sg_ln_g[i][None], sg_ln_b[i][None], sg_w_s[i],
                               sg_b_s[i].T, w_out_odd, g_post[l][None], mgate, nxt, seq)
        x2, h = out if nxt is not None else (out[0], None)
    return x2.reshape(bsz, seq, d)
```

```python
import functools

import jax
import jax.numpy as jnp
from jax import lax
from jax.experimental import pallas as pl
from jax.experimental.pallas import tpu as pltpu

D_MODEL = 2048
DEPTH = 4
CHUNK = 64
EPS = 1e-6
A_HEADS = 8
A_NOPE = 128
A_ROPE = 64
A_VDIM = 128
A_QLORA = 512
A_KVLORA = 256
ROPE_THETA = 10000.0
B_HEADS = 8
B_HDIM = 128
B_PREV = 8
REL_CLIP = 128
SG_GROUPS = 8
SG_LEN = 128
B_WIDTH = B_HEADS * B_HDIM
A_WIDTH = A_HEADS * A_VDIM

VMEM_LIMIT_BYTES = 56 * 1024 * 1024

SUBLANES = 8

LATENT_W = 1024
Z_W = 6144
A_QBLK = 256
ATT_TQ = 256
HEADS_PER_STEP = 4
BAND_TQ = 256
BAND_KEYS = BAND_TQ + B_PREV * CHUNK

BF16 = jnp.bfloat16
F32 = jnp.float32
LOG2E = 1.4426950408889634


def _params(*sem):
    return pltpu.CompilerParams(dimension_semantics=sem, vmem_limit_bytes=VMEM_LIMIT_BYTES)


def _dot(a, b):
    return jnp.dot(a, b, preferred_element_type=F32)


def _dot_nt(a, b):
    return lax.dot_general(a, b, (((1,), (1,)), ((), ())), preferred_element_type=F32)


def _rms(x):
    return x * lax.rsqrt(jnp.mean(x * x, axis=-1, keepdims=True) + EPS)


def _silu(x):
    return x * jax.nn.sigmoid(x)


def _mod_kernel(c_ref, w_ref, b_ref, o_ref):
    cs = _silu(c_ref[...]).astype(BF16)
    o_ref[...] = _dot(cs, w_ref[...].astype(BF16)) + b_ref[...]


def _modulation(c, w_mod, b_mod3, tn=512):
    _, d, n = w_mod.shape
    bsz = c.shape[0]
    return pl.pallas_call(
        _mod_kernel,
        grid=(1, n // tn),
        in_specs=[
            pl.BlockSpec((bsz, d), lambda l, j: (0, 0)),
            pl.BlockSpec((None, d, tn), lambda l, j: (l, 0, j)),
            pl.BlockSpec((None, 1, tn), lambda l, j: (l, 0, j)),
        ],
        out_specs=pl.BlockSpec((None, bsz, tn), lambda l, j: (l, 0, j)),
        out_shape=jax.ShapeDtypeStruct((1, bsz, n), F32),
        compiler_params=_params("arbitrary", "arbitrary"),
        name="modulation",
    )(c, w_mod, b_mod3)


def _adaln(x, g_ref, scale_ref, shift_ref):
    gain = g_ref[...] * (1.0 + scale_ref[...])
    return (_rms(x) * gain + shift_ref[...]).astype(BF16)


def _prenorm_kernel(x_ref, g_ref, scale_ref, shift_ref, h_ref):
    h_ref[...] = _adaln(x_ref[...], g_ref, scale_ref, shift_ref)


def _prenorm(x2, g_pre, scale, shift, seq, tm=1024):
    t, d = x2.shape
    per_b = seq // tm
    vec = pl.BlockSpec((None, 1, d), lambda m: (m // per_b, 0, 0))
    return pl.pallas_call(
        _prenorm_kernel,
        grid=(t // tm,),
        in_specs=[pl.BlockSpec((tm, d), lambda m: (m, 0)),
                  pl.BlockSpec((1, d), lambda m: (0, 0)), vec, vec],
        out_specs=pl.BlockSpec((tm, d), lambda m: (m, 0)),
        out_shape=jax.ShapeDtypeStruct((t, d), BF16),
        compiler_params=_params("arbitrary"),
        name="prenorm",
    )(x2, g_pre, scale, shift)


def _inproj_kernel(transposed, h_ref, w_ref, o_ref, wb_ref):
    @pl.when(pl.program_id(1) == 0)
    def _():
        wb_ref[...] = (w_ref[0].T if transposed else w_ref[...]).astype(BF16)

    o_ref[...] = _dot(h_ref[...], wb_ref[...]).astype(BF16)


def _inproj(h, w_stack, layer, n_out, transposed=False, carry=0, tm=2048, tn=1024):
    t, d = h.shape
    if transposed:
        w_spec = pl.BlockSpec(
            (pl.Element(1), pl.Element(tn), pl.Element(d)),
            lambda n, m: (layer, pl.multiple_of(jnp.maximum(n * tn - carry, 0), SUBLANES), 0))
    else:
        assert carry == 0
        w_spec = pl.BlockSpec((None, d, tn), lambda n, m: (layer, 0, n))
    return pl.pallas_call(
        functools.partial(_inproj_kernel, transposed),
        grid=(n_out // tn, t // tm),
        in_specs=[pl.BlockSpec((tm, d), lambda n, m: (m, 0)), w_spec],
        out_specs=pl.BlockSpec((tm, tn), lambda n, m: (m, n)),
        out_shape=jax.ShapeDtypeStruct((t, n_out), BF16),
        scratch_shapes=[pltpu.VMEM((d, tn), BF16)],
        compiler_params=_params("arbitrary", "arbitrary"),
        name="inproj",
    )(h, w_stack)


def _latent_kernel(z_ref, gq_ref, wq_ref, gkv_ref, wkv_ref, cs_ref,
                   q_ref, kn_ref, v_ref, kr_ref):
    z = z_ref[...].astype(F32)
    cs = cs_ref[...]
    cq = (_rms(z[:, :A_QLORA]) * gq_ref[...]).astype(BF16)
    q = _dot(cq, wq_ref[...])
    qscale = LOG2E * float(A_NOPE + A_ROPE) ** -0.5
    for h in range(A_HEADS):
        lo = h * A_QBLK
        q_ref[:, lo:lo + A_NOPE] = (q[:, lo:lo + A_NOPE] * qscale).astype(BF16)
        pr = q[:, lo + A_NOPE:lo + A_QBLK] * cs
        pr = pr + pltpu.roll(pr, A_ROPE, 1)
        q_ref[:, lo + A_NOPE:lo + A_QBLK] = (pr * qscale).astype(BF16)
    ckv = (_rms(z[:, A_QLORA:A_QLORA + A_KVLORA]) * gkv_ref[...]).astype(BF16)
    kv = _dot(ckv, wkv_ref[...])
    kn_ref[...] = kv[:, :A_WIDTH].astype(BF16)
    v_ref[...] = kv[:, A_WIDTH:].astype(BF16)
    o = A_QLORA + A_KVLORA
    kr = z[:, o:o + 2 * A_ROPE]
    lane = lax.broadcasted_iota(jnp.int32, kr.shape, 1)
    half = A_ROPE // 2
    swapped = jnp.where(lane < half, -pltpu.roll(kr, 2 * A_ROPE - half, 1), pltpu.roll(kr, half, 1))
    roped = kr * cs + swapped * pltpu.roll(cs, A_ROPE, 1)
    kr_ref[...] = jnp.where(lane < A_ROPE, roped, 0.0).astype(BF16)


def _latent(z, g_q, w_uq, g_kv, w_ukv, cs_tab, seq, tm=1024):
    t = z.shape[0]
    per_b = seq // tm
    const = lambda m: (0, 0)
    return pl.pallas_call(
        _latent_kernel,
        grid=(t // tm,),
        in_specs=[
            pl.BlockSpec((tm, LATENT_W), lambda m: (m, 0)),
            pl.BlockSpec((1, A_QLORA), const),
            pl.BlockSpec(w_uq.shape, const),
            pl.BlockSpec((1, A_KVLORA), const),
            pl.BlockSpec(w_ukv.shape, const),
            pl.BlockSpec((tm, 2 * A_ROPE), lambda m: (m % per_b, 0)),
        ],
        out_specs=[
            pl.BlockSpec((tm, A_HEADS * A_QBLK), lambda m: (m, 0)),
            pl.BlockSpec((tm, A_WIDTH), lambda m: (m, 0)),
            pl.BlockSpec((tm, A_WIDTH), lambda m: (m, 0)),
            pl.BlockSpec((tm, 2 * A_ROPE), lambda m: (m, 0)),
        ],
        out_shape=[
            jax.ShapeDtypeStruct((t, A_HEADS * A_QBLK), BF16),
            jax.ShapeDtypeStruct((t, A_WIDTH), BF16),
            jax.ShapeDtypeStruct((t, A_WIDTH), BF16),
            jax.ShapeDtypeStruct((t, 2 * A_ROPE), BF16),
        ],
        compiler_params=_params("arbitrary"),
        name="latent",
    )(z, g_q, w_uq, g_kv, w_ukv, cs_tab)


def _with_ones_column(v):
    lane = lax.broadcasted_iota(jnp.int32, v.shape, 1)
    return jnp.concatenate([v, jnp.where(lane == 0, 1.0, 0.0).astype(v.dtype)], axis=1)


def _softmax_pv(s_parts, v_parts):
    m = functools.reduce(jnp.maximum, [jnp.max(s, axis=-1, keepdims=True) for s in s_parts])
    acc = functools.reduce(
        jnp.add, [_dot(jnp.exp2(s - m).astype(BF16), v) for s, v in zip(s_parts, v_parts)])
    width = acc.shape[1] // 2
    return acc[:, :width] / acc[:, width:width + 1]


def _chunk_causal_mask(n):
    r = lax.broadcasted_iota(jnp.int32, (n, n), 0) // CHUNK
    c = lax.broadcasted_iota(jnp.int32, (n, n), 1) // CHUNK
    return c <= r


MOD_TILE = 1536


def _mla_attn_kernel(mod_steps, q_ref, kn_ref, kr_ref, v_ref, *refs):
    if mod_steps:
        c_ref, wm_ref, bm_ref, o_ref, mo_ref = refs
        step = pl.program_id(0) * pl.num_programs(1) + pl.program_id(1)
        pl.when(step < mod_steps)(lambda: _mod_kernel(c_ref, wm_ref, bm_ref, mo_ref))
    else:
        o_ref, = refs
    seq = q_ref.shape[0]
    kr = kr_ref[...]
    mask = _chunk_causal_mask(ATT_TQ)
    heads = [(jnp.concatenate([kn_ref[:, hh * A_NOPE:(hh + 1) * A_NOPE], kr], axis=1),
              _with_ones_column(v_ref[:, hh * A_VDIM:(hh + 1) * A_VDIM]))
             for hh in range(HEADS_PER_STEP)]
    for i in reversed(range(seq // ATT_TQ)):
        lo = i * ATT_TQ
        for hh, (k, v) in enumerate(heads):
            q = q_ref[lo:lo + ATT_TQ, hh * A_QBLK:(hh + 1) * A_QBLK]
            s_diag = jnp.where(mask, _dot_nt(q, k[lo:lo + ATT_TQ]), -jnp.inf)
            s_parts, v_parts = [s_diag], [v[lo:lo + ATT_TQ]]
            if i > 0:
                s_parts.append(_dot_nt(q, k[:lo]))
                v_parts.append(v[:lo])
            o_ref[lo:lo + ATT_TQ, hh * A_VDIM:(hh + 1) * A_VDIM] = (
                _softmax_pv(s_parts, v_parts).astype(BF16))


def _mla_attention(q, kn, kr, v, mod_job=None):
    bsz, seq, _ = q.shape
    hp = HEADS_PER_STEP
    ng = A_HEADS // hp
    group = lambda b, g: (b, 0, g)
    in_specs = [
        pl.BlockSpec((None, seq, hp * A_QBLK), group),
        pl.BlockSpec((None, seq, hp * A_NOPE), group),
        pl.BlockSpec((None, seq, 2 * A_ROPE), lambda b, g: (b, 0, 0)),
        pl.BlockSpec((None, seq, hp * A_VDIM), group),
    ]
    out_specs = [pl.BlockSpec((None, seq, hp * A_VDIM), group)]
    out_shape = [jax.ShapeDtypeStruct((bsz, seq, A_WIDTH), BF16)]
    args, mod_steps = [q, kn, kr, v], 0
    if mod_job is not None:
        c, w_mod, b_mod3 = mod_job
        depth, d, n = w_mod.shape
        per_layer = n // MOD_TILE
        mod_steps = (depth - 1) * per_layer
        assert mod_steps <= bsz * ng

        def tile(b, g):
            s = jnp.minimum(b * ng + g, mod_steps - 1)
            return 1 + s // per_layer, s % per_layer

        in_specs += [
            pl.BlockSpec(c.shape, lambda b, g: (0, 0)),
            pl.BlockSpec((None, d, MOD_TILE), lambda b, g: (tile(b, g)[0], 0, tile(b, g)[1])),
            pl.BlockSpec((None, 1, MOD_TILE), lambda b, g: (tile(b, g)[0], 0, tile(b, g)[1])),
        ]
        out_specs.append(pl.BlockSpec((None, c.shape[0], MOD_TILE),
                                      lambda b, g: (tile(b, g)[0] - 1, 0, tile(b, g)[1])))
        out_shape.append(jax.ShapeDtypeStruct((depth - 1, c.shape[0], n), F32))
        args += [c, w_mod, b_mod3]
    return pl.pallas_call(
        functools.partial(_mla_attn_kernel, mod_steps),
        grid=(bsz, ng),
        in_specs=in_specs,
        out_specs=out_specs,
        out_shape=out_shape,
        compiler_params=_params("arbitrary", "arbitrary"),
        name="mla_attention",
    )(*args)


def _band_attn_kernel(q_ref, k_ref, v_ref, e_ref, wa_ref, ws_ref, o_ref, wab_ref, wsb_ref, bias_ref):
    wab_ref[...] = wa_ref[...].astype(BF16)
    wsb_ref[...] = ws_ref[...].astype(BF16)
    seq = q_ref.shape[0]
    scale = LOG2E * float(B_HDIM) ** -0.5

    @pl.when(pl.program_id(1) == 0)
    def _():
        for hh in range(HEADS_PER_STEP):
            e = jnp.broadcast_to(e_ref[hh] * LOG2E, (BAND_TQ, e_ref.shape[2]))
            toep = pltpu.roll(e, 0, 1, stride=1, stride_axis=0)[:, :BAND_KEYS]
            qc = lax.broadcasted_iota(jnp.int32, toep.shape, 0) // CHUNK
            kc = lax.broadcasted_iota(jnp.int32, toep.shape, 1) // CHUNK - B_PREV
            bias_ref[hh] = jnp.where((kc <= qc) & (kc >= qc - B_PREV), toep, -jnp.inf)

    vs = [_with_ones_column(v_ref[:, hh * B_HDIM:(hh + 1) * B_HDIM])
          for hh in range(HEADS_PER_STEP)]
    for i in reversed(range(seq // BAND_TQ)):
        lo = i * BAND_TQ
        klo = max(0, lo - B_PREV * CHUNK)
        nk = lo + BAND_TQ - klo
        for hh, v in enumerate(vs):
            cols = slice(hh * B_HDIM, (hh + 1) * B_HDIM)
            q = (q_ref[lo:lo + BAND_TQ, cols].astype(F32) * scale).astype(BF16)
            s = _dot_nt(q, k_ref[klo:klo + nk, cols]) + bias_ref[hh, :, BAND_KEYS - nk:]
            o_ref[lo:lo + BAND_TQ, cols] = _softmax_pv([s], [v[klo:klo + nk]]).astype(BF16)


def _band_attention(z3, e_tab, w_out_a, w_out_s, pair):
    bsz, seq, _ = z3.shape
    hp = HEADS_PER_STEP
    nb = LATENT_W // (hp * B_HDIM)
    ng = B_HEADS // hp
    _, k, n = w_out_a.shape
    slab = k // (ng * bsz)

    def col(off):
        return pl.BlockSpec((None, seq, hp * B_HDIM), lambda g, b: (b, 0, off + g))

    w_in = pl.BlockSpec((None, slab, n), lambda g, b: (pair, g * bsz + b, 0))
    w_out = pl.BlockSpec((slab, n), lambda g, b: (g * bsz + b, 0))
    return pl.pallas_call(
        _band_attn_kernel,
        grid=(ng, bsz),
        in_specs=[
            col(nb), col(nb + ng), col(nb + 2 * ng),
            pl.BlockSpec((hp, 1, e_tab.shape[2]), lambda g, b: (g, 0, 0)),
            w_in, w_in,
        ],
        out_specs=[pl.BlockSpec((None, seq, hp * B_HDIM), lambda g, b: (b, 0, g)), w_out, w_out],
        out_shape=[jax.ShapeDtypeStruct((bsz, seq, B_WIDTH), BF16),
                   jax.ShapeDtypeStruct((k, n), BF16), jax.ShapeDtypeStruct((k, n), BF16)],
        scratch_shapes=[pltpu.VMEM((hp, BAND_TQ, BAND_KEYS), F32)],
        compiler_params=_params("arbitrary", "arbitrary"),
        name="band_attention",
    )(z3, z3, z3, e_tab, w_out_a, w_out_s)


def _band_bias_row(rel_table):
    h = rel_table.shape[0]
    width = 2 * (B_PREV * CHUNK)
    n_hi = B_PREV * CHUNK - REL_CLIP
    n_lo = width - n_hi - (2 * REL_CLIP + 1) - (CHUNK - 1)
    hi = rel_table[:, 2 * REL_CLIP:]
    lo = rel_table[:, :1]
    e = jnp.concatenate([jnp.broadcast_to(hi, (h, n_hi)), rel_table[:, ::-1],
                         jnp.broadcast_to(lo, (h, n_lo)),
                         jnp.broadcast_to(hi, (h, CHUNK - 1))], axis=1)
    return e.astype(F32)[:, None, :]


PIPE_LAG = 1


def _finish(y_fn, refs, emit_h):
    x_ref, wb_ref, gpost_ref, mgate_ref = refs[:4]
    o_ref = refs[7 if emit_h else 4]
    r0_ref, r1_ref = refs[-2:]
    i = pl.program_id(0)

    @pl.when(i == 0)
    def _():
        r1_ref[...] = jnp.zeros_like(r1_ref)

    def step(r_prev, r_cur):
        xn = x_ref[...] + _rms(r_prev[...]) * (mgate_ref[...] * gpost_ref[...])
        o_ref[...] = xn
        if emit_h:
            refs[8][...] = _adaln(xn, *refs[4:7])
        r_cur[...] = _dot(y_fn(r_cur).astype(BF16), wb_ref[...])

    pl.when(i % 2 == 0)(lambda: step(r1_ref, r0_ref))
    pl.when(i % 2 == 1)(lambda: step(r0_ref, r1_ref))


def _finish_specs(x2, w_bf16, g_post, mgate, nxt, seq, tm):
    t, d = x2.shape
    per_b = seq // tm
    prev = lambda i: jnp.maximum(i - PIPE_LAG, 0)
    row = pl.BlockSpec((tm, d), lambda i: (prev(i), 0))
    vec = pl.BlockSpec((1, d), lambda i: (0, 0))
    bvec = pl.BlockSpec((None, 1, d), lambda i: (prev(i) // per_b, 0, 0))
    whole = pl.BlockSpec((d, d), lambda i: (0, 0), pipeline_mode=pl.Buffered(1))
    args, in_specs = [x2, w_bf16, g_post, mgate], [row, whole, vec, bvec]
    out_specs, out_shape = [row], [jax.ShapeDtypeStruct((t, d), F32)]
    if nxt is not None:
        args += list(nxt)
        in_specs += [vec, bvec, bvec]
        out_specs.append(row)
        out_shape.append(jax.ShapeDtypeStruct((t, d), BF16))
    scratch = [pltpu.VMEM((tm, d), F32), pltpu.VMEM((tm, d), F32)]
    return args, in_specs, out_specs, out_shape, scratch


def _outproj_even_kernel(emit_h, oa_ref, ob_ref, gate_ref, *refs):
    def y_fn(_):
        o = jnp.concatenate([oa_ref[...], ob_ref[...]], axis=1).astype(F32)
        return o * _silu(gate_ref[...].astype(F32))

    _finish(y_fn, refs, emit_h)


def _outproj_even(oa, ob, z, x2, w_bf16, g_post, mgate, nxt, seq, tm=512):
    t, d = x2.shape
    n_tiles = t // tm
    cur = lambda i: jnp.minimum(i, n_tiles - 1)
    args, in_specs, out_specs, out_shape, scratch = _finish_specs(
        x2, w_bf16, g_post, mgate, nxt, seq, tm)
    return pl.pallas_call(
        functools.partial(_outproj_even_kernel, nxt is not None),
        grid=(n_tiles + PIPE_LAG,),
        in_specs=[
            pl.BlockSpec((tm, A_WIDTH), lambda i: (cur(i), 0)),
            pl.BlockSpec((tm, B_WIDTH), lambda i: (cur(i), 0)),
            pl.BlockSpec((tm, d), lambda i: (cur(i), 2)),
        ] + in_specs,
        out_specs=out_specs,
        out_shape=out_shape,
        scratch_shapes=scratch,
        compiler_params=_params("arbitrary"),
        name="outproj_even",
    )(oa, ob, z, *args)


def _sgu_kernel(emit_h, u_ref, v_ref, gate_ref, lng_ref, lnb_ref, ws_ref, bs_ref, *refs):
    tm, d = v_ref.shape
    dg = d // SG_GROUPS

    def y_fn(sv_ref):
        v = v_ref[...].astype(F32)
        mu = jnp.mean(v, axis=-1, keepdims=True)
        vc = v - mu
        var = jnp.mean(vc * vc, axis=-1, keepdims=True)
        vn = ((vc * lax.rsqrt(var + EPS)) * lng_ref[...] + lnb_ref[...]).astype(BF16)
        cpos_r = lax.broadcasted_iota(jnp.int32, (SG_LEN, SG_LEN), 0) // CHUNK
        cpos_c = lax.broadcasted_iota(jnp.int32, (SG_LEN, SG_LEN), 1) // CHUNK
        mask = cpos_c <= cpos_r
        for g in range(SG_GROUPS):
            ws = jnp.where(mask, ws_ref[g], 0.0).astype(BF16)
            bs = bs_ref[:, g:g + 1]
            for n in range(tm // SG_LEN):
                blk = vn[n * SG_LEN:(n + 1) * SG_LEN, g * dg:(g + 1) * dg]
                sv_ref[n * SG_LEN:(n + 1) * SG_LEN, g * dg:(g + 1) * dg] = _dot(ws, blk) + bs
        return u_ref[...].astype(F32) * sv_ref[...] * _silu(gate_ref[...].astype(F32))

    _finish(y_fn, refs, emit_h)


def _sgu_outproj(z, x2, ln_g, ln_b, w_s, b_s_t, w_bf16, g_post, mgate, nxt, seq, tm=512):
    t, d = x2.shape
    n_tiles = t // tm
    cur = lambda i: jnp.minimum(i, n_tiles - 1)
    const2 = lambda i: (0, 0)
    args, in_specs, out_specs, out_shape, scratch = _finish_specs(
        x2, w_bf16, g_post, mgate, nxt, seq, tm)
    return pl.pallas_call(
        functools.partial(_sgu_kernel, nxt is not None),
        grid=(n_tiles + PIPE_LAG,),
        in_specs=[
            pl.BlockSpec((tm, d), lambda i: (cur(i), 0)),
            pl.BlockSpec((tm, d), lambda i: (cur(i), 1)),
            pl.BlockSpec((tm, d), lambda i: (cur(i), 2)),
            pl.BlockSpec((1, d), const2),
            pl.BlockSpec((1, d), const2),
            pl.BlockSpec(w_s.shape, lambda i: (0, 0, 0)),
            pl.BlockSpec(b_s_t.shape, const2),
        ] + in_specs,
        out_specs=out_specs,
        out_shape=out_shape,
        scratch_shapes=scratch,
        compiler_params=_params("arbitrary"),
        name="sgu_outproj",
    )(z, z, z, ln_g, ln_b, w_s, b_s_t, *args)


def _half_swap(w):
    half = w.shape[-1] // 2
    return jnp.concatenate([-w[..., half:], w[..., :half]], axis=-1)


def _prep_latent_weights(w_uq, w_ukv):
    wq = w_uq.reshape(A_QLORA, A_HEADS, A_NOPE + A_ROPE)
    wq_rope = wq[..., A_NOPE:]
    wq_p = jnp.concatenate([wq, _half_swap(wq_rope)], axis=-1).reshape(A_QLORA, A_HEADS * A_QBLK)
    wkv = w_ukv.reshape(A_KVLORA, A_HEADS, A_NOPE + A_VDIM)
    wkv_p = jnp.concatenate([wkv[..., :A_NOPE].reshape(A_KVLORA, A_WIDTH),
                             wkv[..., A_NOPE:].reshape(A_KVLORA, A_WIDTH)], axis=1)
    return wq_p.astype(BF16), wkv_p.astype(BF16)


def _rope_table(seq):
    half = A_ROPE // 2
    pos = jnp.arange(seq, dtype=F32)
    freqs = ROPE_THETA ** (-jnp.arange(half, dtype=F32) / half)
    ang = pos[:, None] * freqs[None, :]
    cos, sin = jnp.cos(ang), jnp.sin(ang)
    return jnp.concatenate([cos, cos, sin, sin], axis=1)


def kernel(x, c, w_mod, b_mod, g_pre, g_post, ab_w_in, a_g_q, a_w_uq, a_g_kv, a_w_ukv,
           b_rel_bias, ab_w_out, sg_w_in, sg_ln_g, sg_ln_b, sg_w_s, sg_b_s, sg_w_out):
    bsz, seq, d = x.shape
    t = bsz * seq
    assert x.dtype == F32 and d == D_MODEL == A_WIDTH + B_WIDTH and w_mod.shape[0] == DEPTH
    assert ab_w_in.shape[2] + LATENT_W - (A_QLORA + A_KVLORA + A_ROPE) == Z_W
    b_mod3 = b_mod.reshape(DEPTH, 1, 3 * d)
    mods = [_modulation(c, w_mod, b_mod3)[0]]
    cs_tab = _rope_table(seq)
    x2 = x.reshape(t, d)

    def adaln_args(l):
        return g_pre[l][None], mods[l][:, None, d:2 * d], mods[l][:, None, :d]

    ab_w_in_t = jnp.swapaxes(ab_w_in, 1, 2)
    h = _prenorm(x2, *adaln_args(0), seq)
    for l in range(DEPTH):
        mgate = mods[l][:, None, 2 * d:]
        i = l // 2
        if l % 2 == 0:
            w_uq, w_ukv = _prep_latent_weights(a_w_uq[i], a_w_ukv[i])
            z = _inproj(h, ab_w_in_t, i, Z_W, transposed=True,
                        carry=LATENT_W - (A_QLORA + A_KVLORA + A_ROPE))
            q, kn, v, kr = _latent(z, a_g_q[i][None], w_uq, a_g_kv[i][None], w_ukv, cs_tab, seq)
            att = _mla_attention(q.reshape(bsz, seq, -1), kn.reshape(bsz, seq, -1),
                                 kr.reshape(bsz, seq, -1), v.reshape(bsz, seq, -1),
                                 mod_job=(c, w_mod, b_mod3) if l == 0 else None)
            oa = att[0]
            if l == 0:
                mods += list(att[1])
            ob, w_out_even, w_out_odd = _band_attention(
                z.reshape(bsz, seq, Z_W), _band_bias_row(b_rel_bias[i]), ab_w_out, sg_w_out, i)
            nxt = adaln_args(l + 1) if l + 1 < DEPTH else None
            out = _outproj_even(oa.reshape(t, A_WIDTH), ob.reshape(t, B_WIDTH), z, x2,
                                w_out_even, g_post[l][None], mgate, nxt, seq)
        else:
            nxt = adaln_args(l + 1) if l + 1 < DEPTH else None
            z = _inproj(h, sg_w_in, i, sg_w_in.shape[2])
            out = _sgu_outproj(z, x2, sg_ln_g[i][None], sg_ln_b[i][None], sg_w_s[i],
                               sg_b_s[i].T, w_out_odd, g_post[l][None], mgate, nxt, seq)
        x2, h = out if nxt is not None else (out[0], None)
    return x2.reshape(bsz, seq, d)
```
